```python
import jax, jax.numpy as jnp
from jax import lax
import numpy as np

D_MODEL = 1024
BATCH = 4
SEQ = 4096
DEPTH = 1
DEC_BATCH = 32
DEC_SEQ = 16
PAST_LEN = 4096

CHUNK = 64
POOL_WINDOWS = (2, 4, 8, 16)
POOL_GROUPS = len(POOL_WINDOWS)
POOL_WIDTH = D_MODEL // 2
POOL_GROUP_DIM = POOL_WIDTH // POOL_GROUPS
POOL_HIST = max(POOL_WINDOWS) - 1
CONV_DIM = D_MODEL
CONV_WIDTH = 31
CONV_HIST = CONV_WIDTH - 1
IN_WIDTH = POOL_WIDTH + 2 * CONV_DIM + 2 * D_MODEL
N_EXPERTS = 32
TOP_K = 4
D_FF = D_MODEL
SWIGLU_LIMIT = 7.0
SWIGLU_ALPHA = 1.702
NORM_EPS = 1e-6

kernel_name = "hybrid_pool_conformer_moe_stream_step"


def rmsnorm(x, g):
    x32 = x.astype(jnp.float32)
    y = x32 * lax.rsqrt(jnp.mean(x32 * x32, axis=-1, keepdims=True) + NORM_EPS)
    return (y * g.astype(jnp.float32)).astype(x.dtype)


def layernorm(x, g, b):
    x32 = x.astype(jnp.float32)
    mu = jnp.mean(x32, axis=-1, keepdims=True)
    var = jnp.mean(jnp.square(x32 - mu), axis=-1, keepdims=True)
    y = (x32 - mu) * lax.rsqrt(var + NORM_EPS) * g.astype(jnp.float32) + b.astype(jnp.float32)
    return y.astype(x.dtype)


def pool_mixer(zp, hist, pos0, w_pool_grp, pool_scale):
    bsz, seqlen, _ = zp.shape
    ext = jnp.concatenate([hist.astype(zp.dtype), zp], axis=1)
    new_state = ext[:, -POOL_HIST:]
    ext32 = ext.astype(jnp.float32)
    cs = jnp.pad(jnp.cumsum(ext32, axis=1), ((0, 0), (1, 0), (0, 0)))
    pos = (pos0 + jnp.arange(seqlen)).astype(jnp.float32)
    end = cs[:, POOL_HIST + 1:POOL_HIST + 1 + seqlen]
    means = []
    for g, w in enumerate(POOL_WINDOWS):
        sl = slice(g * POOL_GROUP_DIM, (g + 1) * POOL_GROUP_DIM)
        start = cs[:, POOL_HIST + 1 - w:POOL_HIST + 1 - w + seqlen, sl]
        cnt = jnp.minimum(pos + 1.0, float(w))[None, :, None]
        means.append((end[..., sl] - start) / cnt)
    d = jnp.concatenate(means, axis=-1) - ext32[:, POOL_HIST:]
    d = d.reshape(bsz, seqlen, POOL_GROUPS, POOL_GROUP_DIM)
    y = jnp.einsum('blgc,gcd->blgd', d, w_pool_grp.astype(jnp.float32))
    y = y.reshape(bsz, seqlen, POOL_WIDTH) * pool_scale.astype(jnp.float32)
    return y.astype(zp.dtype), new_state


def conformer_conv(zc, hist, dw_w, dw_b, ln_g, ln_b):
    a, gte = jnp.split(zc, 2, axis=-1)
    u = a * jax.nn.sigmoid(gte)
    ext = jnp.concatenate([hist.astype(u.dtype), u], axis=1)
    new_state = ext[:, -CONV_HIST:]
    v = lax.conv_general_dilated(
        ext, dw_w[:, None, :].astype(ext.dtype), window_strides=(1,), padding='VALID',
        dimension_numbers=('NWC', 'WIO', 'NWC'), feature_group_count=CONV_DIM)
    v = v + dw_b
    v = layernorm(v, ln_g, ln_b)
    return jax.nn.silu(v), new_state


def moe(xn, w_router, b_router, w_gate_up, b_gate_up, w_down, b_down):
    bsz, seqlen, dm = xn.shape
    t = xn.reshape(-1, dm)
    logits = (t @ w_router).astype(jnp.float32) + b_router.astype(jnp.float32)
    top_v, top_i = lax.top_k(logits, TOP_K)
    wts = jax.nn.softmax(top_v, axis=-1)
    comb = jnp.einsum('nk,nke->ne', wts, jax.nn.one_hot(top_i, N_EXPERTS, dtype=jnp.float32))
    out = jnp.zeros((t.shape[0], dm), jnp.float32)
    for e in range(N_EXPERTS):
        gu = t @ w_gate_up[e] + b_gate_up[e]
        gate = jnp.minimum(gu[:, :D_FF], SWIGLU_LIMIT)
        up = jnp.clip(gu[:, D_FF:], -SWIGLU_LIMIT, SWIGLU_LIMIT)
        h = (up + 1.0) * (gate * jax.nn.sigmoid(SWIGLU_ALPHA * gate))
        out = out + comb[:, e:e + 1] * (h @ w_down[e] + b_down[e]).astype(jnp.float32)
    return out.reshape(bsz, seqlen, dm).astype(xn.dtype)


def trunk_layer(x, pool_hist, conv_hist, pos0, norm1_g, w_in, w_pool_grp, pool_scale, w_pool_out,
                dw_w, dw_b, conv_ln_g, conv_ln_b, w_conv_out, w_out, norm2_g, w_router, b_router,
                w_gate_up, b_gate_up, w_down, b_down):
    n1 = rmsnorm(x, norm1_g)
    z = n1 @ w_in
    zp = z[..., :POOL_WIDTH]
    zc = z[..., POOL_WIDTH:POOL_WIDTH + 2 * CONV_DIM]
    zg = z[..., POOL_WIDTH + 2 * CONV_DIM:]
    a, new_pool = pool_mixer(zp, pool_hist, pos0, w_pool_grp, pool_scale)
    a = a @ w_pool_out
    b, new_conv = conformer_conv(zc, conv_hist, dw_w, dw_b, conv_ln_g, conv_ln_b)
    b = b @ w_conv_out
    g_a, g_b = jnp.split(jax.nn.sigmoid(zg), 2, axis=-1)
    h = x + (g_a * a + g_b * b) @ w_out
    y = h + moe(rmsnorm(h, norm2_g), w_router, b_router, w_gate_up, b_gate_up, w_down, b_down)
    return y, new_pool, new_conv


def run_stack(x, pool_states, conv_states, pos0, layer_params, final_g):
    pools, convs = [], []
    for l in range(DEPTH):
        x, np_, nc_ = trunk_layer(x, pool_states[l], conv_states[l], pos0,
                                  *[p[l] for p in layer_params])
        pools.append(np_)
        convs.append(nc_)
    return rmsnorm(x, final_g), jnp.stack(pools, 0), jnp.stack(convs, 0)


def setup_inputs(seed: int = 0) -> dict:
    key = jax.random.key(seed)
    ks = jax.random.split(key, 24)
    f32 = jnp.float32
    nrm = lambda k, shape, scale: jax.random.normal(k, shape, f32) * scale
    return {
        "x_prompt": nrm(ks[0], (BATCH, SEQ, D_MODEL), 1.0),
        "x_sample": nrm(ks[1], (DEC_BATCH, DEC_SEQ, D_MODEL), 1.0),
        "state_pool": nrm(ks[2], (DEPTH, DEC_BATCH, POOL_HIST, POOL_WIDTH), 1.0),
        "state_conv": nrm(ks[3], (DEPTH, DEC_BATCH, CONV_HIST, CONV_DIM), 0.5),
        "norm1_g": 1.0 + nrm(ks[4], (DEPTH, D_MODEL), 0.05),
        "w_in": nrm(ks[5], (DEPTH, D_MODEL, IN_WIDTH), D_MODEL ** -0.5),
        "w_pool_grp": nrm(ks[6], (DEPTH, POOL_GROUPS, POOL_GROUP_DIM, POOL_GROUP_DIM), POOL_GROUP_DIM ** -0.5),
        "pool_scale": 1.0 + nrm(ks[7], (DEPTH, POOL_WIDTH), 0.1),
        "w_pool_out": nrm(ks[8], (DEPTH, POOL_WIDTH, D_MODEL), POOL_WIDTH ** -0.5),
        "dw_w": nrm(ks[9], (DEPTH, CONV_WIDTH, CONV_DIM), CONV_WIDTH ** -0.5),
        "dw_b": nrm(ks[10], (DEPTH, CONV_DIM), 0.02),
        "conv_ln_g": 1.0 + nrm(ks[11], (DEPTH, CONV_DIM), 0.05),
        "conv_ln_b": nrm(ks[12], (DEPTH, CONV_DIM), 0.02),
        "w_conv_out": nrm(ks[13], (DEPTH, CONV_DIM, D_MODEL), CONV_DIM ** -0.5),
        "w_out": nrm(ks[14], (DEPTH, D_MODEL, D_MODEL), D_MODEL ** -0.5),
        "norm2_g": 1.0 + nrm(ks[15], (DEPTH, D_MODEL), 0.05),
        "w_router": nrm(ks[16], (DEPTH, D_MODEL, N_EXPERTS), D_MODEL ** -0.5),
        "b_router": nrm(ks[17], (DEPTH, N_EXPERTS), 0.01),
        "w_gate_up": nrm(ks[18], (DEPTH, N_EXPERTS, D_MODEL, 2 * D_FF), D_MODEL ** -0.5),
        "b_gate_up": nrm(ks[19], (DEPTH, N_EXPERTS, 2 * D_FF), 0.01),
        "w_down": nrm(ks[20], (DEPTH, N_EXPERTS, D_FF, D_MODEL), D_FF ** -0.5),
        "b_down": nrm(ks[21], (DEPTH, N_EXPERTS, D_MODEL), 0.01),
        "final_g": 1.0 + nrm(ks[22], (D_MODEL,), 0.05),
    }


def reference(x_prompt, x_sample, state_pool, state_conv, norm1_g, w_in, w_pool_grp, pool_scale,
              w_pool_out, dw_w, dw_b, conv_ln_g, conv_ln_b, w_conv_out, w_out, norm2_g,
              w_router, b_router, w_gate_up, b_gate_up, w_down, b_down, final_g):
    layer_params = (norm1_g, w_in, w_pool_grp, pool_scale, w_pool_out, dw_w, dw_b, conv_ln_g,
                    conv_ln_b, w_conv_out, w_out, norm2_g, w_router, b_router, w_gate_up,
                    b_gate_up, w_down, b_down)
    bsz = x_prompt.shape[0]
    zero_pool = jnp.zeros((DEPTH, bsz, POOL_HIST, POOL_WIDTH), x_prompt.dtype)
    zero_conv = jnp.zeros((DEPTH, bsz, CONV_HIST, CONV_DIM), x_prompt.dtype)
    y_prompt, new_pool_prompt, new_conv_prompt = run_stack(
        x_prompt, zero_pool, zero_conv, 0, layer_params, final_g)
    y_sample, new_pool_sample, new_conv_sample = run_stack(
        x_sample, state_pool, state_conv, PAST_LEN, layer_params, final_g)
    return (y_prompt, y_sample, new_pool_prompt, new_conv_prompt, new_pool_sample, new_conv_sample)
```

```python
import functools

import jax
import jax.numpy as jnp
from jax import lax
from jax.experimental import pallas as pl
from jax.experimental.pallas import tpu as pltpu

F32 = jnp.float32
BF16 = jnp.bfloat16
I32 = jnp.int32

D_MODEL = 1024
POOL_WINDOWS = (2, 4, 8, 16)
POOL_WIDTH = 512
POOL_GROUP_DIM = 128
POOL_HIST = 15
CONV_DIM = 1024
CONV_WIDTH = 31
CONV_HIST = 30
N_EXPERTS = 32
TOP_K = 4
D_FF = 1024
SWIGLU_LIMIT = 7.0
SWIGLU_ALPHA = 1.702
NORM_EPS = 1e-6

SUBLANES = 8
POOL_BASE = 16
CONV_BASE = 32
META_LANES = 128
META_RANK = 4
META_WEIGHT = 8

FRONT_TILE = 256
SAMPLE_STREAMS = 8
TOKEN_TILE = 256
EXPERT_TILE = 256
CONV_CHUNK_ROWS = 32
VMEM_LIMIT = 56 * 1024 * 1024


def _sigmoid(v):
    return 1.0 / (1.0 + jnp.exp(-v))


def _dot(a, b):
    return jnp.dot(a, b, preferred_element_type=F32)


def _front_kernel(S, Lt, pos0,
                  x_ref, hp_ref, hc_ref, base_ref, g1_ref, win_ref, wg_ref, ps_ref, wpo_ref,
                  dww_ref, dwb_ref, lng_ref, lnb_ref, wco_ref, wo_ref, g2_ref, wr_ref, br_ref,
                  h_ref, xn_ref, meta_ref, np_ref, nc_ref, cnt_ref,
                  ep_ref, ec_ref, sb_ref, run_ref):
    b = pl.program_id(0)
    l = pl.program_id(1)
    T = S * Lt
    P, C, D = POOL_WIDTH, CONV_DIM, D_MODEL
    tap0 = CONV_BASE - CONV_HIST

    @pl.when(l == 0)
    def _():
        ep_ref[:, POOL_BASE - POOL_HIST:POOL_BASE, :] = hp_ref[...]
        for r in range(SUBLANES):
            lo = max(0, tap0 - r)
            ec_ref[r, :, lo:CONV_BASE - r, :] = hc_ref[:, lo + r - tap0:CONV_HIST, :]

    @pl.when((b == 0) & (l == 0))
    def _():
        run_ref[...] = base_ref[...]

    x = x_ref[...].reshape(T, D)
    n1 = x * lax.rsqrt(jnp.mean(x * x, axis=-1, keepdims=True) + NORM_EPS) * g1_ref[...]
    n1b = n1.astype(BF16)

    zp = _dot(n1b, win_ref[:, 0:P])
    ep_ref[:, POOL_BASE:POOL_BASE + Lt, :] = zp.reshape(S, Lt, P)
    pos = (pos0 + l * Lt + lax.broadcasted_iota(I32, (1, Lt, 1), 1)).astype(F32)
    ys = []
    for g, w in enumerate(POOL_WINDOWS):
        sl = slice(g * POOL_GROUP_DIM, (g + 1) * POOL_GROUP_DIM)
        cur = ep_ref[:, POOL_BASE:POOL_BASE + Lt, sl]
        acc = cur
        for j in range(1, w):
            acc = acc + ep_ref[:, POOL_BASE - j:POOL_BASE - j + Lt, sl]
        cnt = jnp.minimum(pos + 1.0, float(w))
        d = acc / cnt - cur
        ys.append(_dot(d.reshape(T, POOL_GROUP_DIM).astype(BF16), wg_ref[g]))
    yp = jnp.concatenate(ys, axis=-1) * ps_ref[...]
    a = _dot(yp.astype(BF16), wpo_ref[...])
    new_pool = ep_ref[:, Lt + POOL_BASE - POOL_HIST:Lt + POOL_BASE, :]
    np_ref[0] = new_pool
    ep_ref[:, POOL_BASE - POOL_HIST:POOL_BASE, :] = new_pool

    za = _dot(n1b, win_ref[:, P:P + C])
    zg = _dot(n1b, win_ref[:, P + C:P + 2 * C])
    u = za * _sigmoid(zg)
    u3 = u.reshape(S, Lt, C)
    for r in range(SUBLANES):
        ec_ref[r, :, CONV_BASE - r:CONV_BASE - r + Lt, :] = u3
    rc = min(CONV_CHUNK_ROWS, Lt)
    sc = CONV_CHUNK_ROWS // rc
    for s0 in range(0, S, sc):
        for r0 in range(0, Lt, rc):
            acc = None
            for j in range(CONV_WIDTH):
                r = (tap0 + j) % SUBLANES
                base = tap0 + j - r + r0
                term = ec_ref[r, s0:s0 + sc, base:base + rc, :] * dww_ref[j:j + 1, :]
                acc = term if acc is None else acc + term
            v = acc + dwb_ref[...]
            mu = jnp.mean(v, axis=-1, keepdims=True)
            vc = v - mu
            var = jnp.mean(vc * vc, axis=-1, keepdims=True)
            vn = vc * lax.rsqrt(var + NORM_EPS) * lng_ref[...] + lnb_ref[...]
            act = vn * _sigmoid(vn)
            row = s0 * Lt + r0
            sb_ref[row:row + sc * rc, :] = act.reshape(sc * rc, C).astype(BF16)
    nc_ref[0] = ec_ref[tap0, :, Lt:Lt + CONV_HIST, :]
    for r in range(SUBLANES):
        lo = max(0, tap0 - r)
        ec_ref[r, :, lo:CONV_BASE - r, :] = ec_ref[r, :, Lt + lo:Lt + CONV_BASE - r, :]
    bb = _dot(sb_ref[...], wco_ref[...])

    ga = _sigmoid(_dot(n1b, win_ref[:, P + 2 * C:P + 2 * C + D]))
    gb = _sigmoid(_dot(n1b, win_ref[:, P + 2 * C + D:P + 2 * C + 2 * D]))
    m = ga * a + gb * bb
    h = x + _dot(m.astype(BF16), wo_ref[...])
    h_ref[...] = h.reshape(S, Lt, D)
    xn = h * lax.rsqrt(jnp.mean(h * h, axis=-1, keepdims=True) + NORM_EPS) * g2_ref[...]
    xn_ref[...] = xn.reshape(S, Lt, D)

    logits = _dot(xn.astype(BF16), wr_ref[...]) + br_ref[...]
    lane_e = lax.broadcasted_iota(I32, (T, N_EXPERTS), 1).astype(F32)
    work = logits
    sels, tops, ids = [], [], []
    for _ in range(TOP_K):
        mx = jnp.max(work, axis=-1, keepdims=True)
        idx = jnp.min(jnp.where(work == mx, lane_e, float(N_EXPERTS)), axis=-1, keepdims=True)
        sel = lane_e == idx
        work = jnp.where(sel, -jnp.inf, work)
        sels.append(sel)
        tops.append(mx)
        ids.append(idx)
    exps = [jnp.exp(t - tops[0]) for t in tops]
    denom = exps[0] + exps[1] + exps[2] + exps[3]
    wts = [e / denom for e in exps]

    mask = jnp.zeros((T, N_EXPERTS), F32)
    for sel in sels:
        mask = mask + sel.astype(F32)
    tri = (lax.broadcasted_iota(I32, (T, T), 1) < lax.broadcasted_iota(I32, (T, T), 0)).astype(BF16)
    excl = _dot(tri, mask.astype(BF16)) + run_ref[...]
    run_new = run_ref[...] + jnp.sum(mask, axis=0, keepdims=True)
    run_ref[...] = run_new
    cnt_ref[...] = run_new

    lane_m = lax.broadcasted_iota(I32, (T, META_LANES), 1)
    meta = jnp.zeros((T, META_LANES), F32)
    for k in range(TOP_K):
        rank = jnp.sum(jnp.where(sels[k], excl, 0.0), axis=-1, keepdims=True)
        meta = jnp.where(lane_m == k, ids[k], meta)
        meta = jnp.where(lane_m == META_RANK + k, rank, meta)
        meta = jnp.where(lane_m == META_WEIGHT + k, wts[k], meta)
    meta_ref[...] = meta.reshape(S, Lt, META_LANES)


def _const_spec(shape):
    nd = len(shape)
    return pl.BlockSpec(shape, lambda b, l: (0,) * nd, pipeline_mode=pl.Buffered(1))


def _front(x, hist_pool, hist_conv, base_cnt, weights, S, Lt, pos0):
    B, L, D = x.shape
    P, C = POOL_WIDTH, CONV_DIM
    grid = (B // S, L // Lt)
    T = S * Lt
    tok = lambda b, l: (b, l, 0)
    per_b = lambda b, l: (b, 0, 0)
    in_specs = [
        pl.BlockSpec((S, Lt, D), tok),
        pl.BlockSpec((S, POOL_HIST, P), per_b),
        pl.BlockSpec((S, CONV_HIST, C), per_b),
        _const_spec((1, N_EXPERTS)),
    ] + [_const_spec(w.shape) for w in weights]
    out_shape = (
        jax.ShapeDtypeStruct((B, L, D), F32),
        jax.ShapeDtypeStruct((B, L, D), F32),
        jax.ShapeDtypeStruct((B, L, META_LANES), F32),
        jax.ShapeDtypeStruct((1, B, POOL_HIST, P), F32),
        jax.ShapeDtypeStruct((1, B, CONV_HIST, C), F32),
        jax.ShapeDtypeStruct((1, N_EXPERTS), F32),
    )
    out_specs = (
        pl.BlockSpec((S, Lt, D), tok),
        pl.BlockSpec((S, Lt, D), tok),
        pl.BlockSpec((S, Lt, META_LANES), tok),
        pl.BlockSpec((1, S, POOL_HIST, P), lambda b, l: (0, b, 0, 0)),
        pl.BlockSpec((1, S, CONV_HIST, C), lambda b, l: (0, b, 0, 0)),
        pl.BlockSpec((1, N_EXPERTS), lambda b, l: (0, 0)),
    )
    scratch = [
        pltpu.VMEM((S, POOL_BASE + Lt, P), F32),
        pltpu.VMEM((SUBLANES, S, CONV_BASE + Lt, C), F32),
        pltpu.VMEM((T, C), BF16),
        pltpu.VMEM((1, N_EXPERTS), F32),
    ]
    return pl.pallas_call(
        functools.partial(_front_kernel, S, Lt, pos0),
        grid=grid, in_specs=in_specs, out_specs=out_specs, out_shape=out_shape,
        scratch_shapes=scratch,
        compiler_params=pltpu.CompilerParams(
            dimension_semantics=("arbitrary", "arbitrary"), vmem_limit_bytes=VMEM_LIMIT),
        name="front",
    )(x, hist_pool, hist_conv, base_cnt, *weights)


def _row_copy(src_ref, src_row, dst_ref, dst_row, sem):
    return pltpu.make_async_copy(src_ref.at[pl.ds(src_row, 1)], dst_ref.at[pl.ds(dst_row, 1)], sem)


def _dispatch_kernel(T, steps_a, n_tiles, pos_ref, zrow_ref, used_ref, xa_ref, xb_ref, xs_ref,
                     sem, zbuf, zsem):
    i = pl.program_id(0)

    @pl.when(i == 0)
    def _():
        zbuf[...] = jnp.zeros_like(zbuf)

        def fill(row0):
            row0 = pl.multiple_of(row0, EXPERT_TILE)
            return pltpu.make_async_copy(zbuf, xs_ref.at[pl.ds(row0, EXPERT_TILE)], zsem)

        def tail_start(j, carry):
            fill(j * EXPERT_TILE).start()
            return carry

        def tail_wait(j, carry):
            fill(j * EXPERT_TILE).wait()
            return carry

        for e in range(N_EXPERTS):
            pl.when(zrow_ref[e] >= 0)(lambda e=e: fill(zrow_ref[e]).start())
        lax.fori_loop(used_ref[0], n_tiles, tail_start, 0)
        for e in range(N_EXPERTS):
            pl.when(zrow_ref[e] >= 0)(lambda e=e: fill(zrow_ref[e]).wait())
        lax.fori_loop(used_ref[0], n_tiles, tail_wait, 0)

    def scatter(src_ref):
        def start(t, carry):
            for k in range(TOP_K):
                _row_copy(src_ref, t, xs_ref, pos_ref[TOP_K * t + k], sem).start(priority=k % 2)
            return carry

        def wait(t, carry):
            for k in range(TOP_K):
                _row_copy(src_ref, 0, xs_ref, 0, sem).wait()
            return carry

        lax.fori_loop(0, T, start, 0, unroll=8)
        lax.fori_loop(0, T, wait, 0, unroll=8)

    pl.when(i < steps_a)(lambda: scatter(xa_ref))
    pl.when(i >= steps_a)(lambda: scatter(xb_ref))


def _dispatch(pos_flat, zero_rows, used_tiles, xn_a, xn_b, n_tiles):
    D = xn_a.shape[1]
    T = TOKEN_TILE
    steps_a, steps_b = xn_a.shape[0] // T, xn_b.shape[0] // T
    return pl.pallas_call(
        functools.partial(_dispatch_kernel, T, steps_a, n_tiles),
        grid=(steps_a + steps_b,),
        in_specs=[
            pl.BlockSpec((T * TOP_K,), lambda i: (i,), memory_space=pltpu.SMEM),
            pl.BlockSpec(memory_space=pltpu.SMEM),
            pl.BlockSpec(memory_space=pltpu.SMEM),
            pl.BlockSpec((T, D), lambda i: (jnp.minimum(i, steps_a - 1), 0)),
            pl.BlockSpec((T, D), lambda i: (jnp.maximum(i - steps_a, 0), 0)),
        ],
        out_specs=pl.BlockSpec(memory_space=pl.ANY),
        out_shape=jax.ShapeDtypeStruct((n_tiles * EXPERT_TILE, D), F32),
        scratch_shapes=[pltpu.SemaphoreType.DMA(()), pltpu.VMEM((EXPERT_TILE, D), F32),
                        pltpu.SemaphoreType.DMA(())],
        compiler_params=pltpu.CompilerParams(dimension_semantics=("arbitrary",)),
        name="dispatch",
    )(pos_flat, zero_rows, used_tiles, xn_a, xn_b)


def _expert_kernel(te_ref, on_ref, last_ref, xs_ref, wgu_ref, bgu_ref, wd_ref, bd_ref, ys_ref,
                   wgu_bf, wd_bf):
    del last_ref
    i = pl.program_id(0)
    prev = te_ref[jnp.maximum(i - 1, 0)]

    @pl.when((i == 0) | (te_ref[i] != prev))
    def _():
        wgu_bf[...] = wgu_ref[0].astype(BF16)
        wd_bf[...] = wd_ref[0].astype(BF16)

    @pl.when(on_ref[i] == 1)
    def _():
        xb = xs_ref[...].astype(BF16)
        gu = _dot(xb, wgu_bf[...]) + bgu_ref[0]
        gate = jnp.minimum(gu[:, :D_FF], SWIGLU_LIMIT)
        up = jnp.clip(gu[:, D_FF:], -SWIGLU_LIMIT, SWIGLU_LIMIT)
        hm = (up + 1.0) * (gate * _sigmoid(SWIGLU_ALPHA * gate))
        ys_ref[...] = _dot(hm.astype(BF16), wd_bf[...]) + bd_ref[0]

    @pl.when(on_ref[i] == 0)
    def _():
        ys_ref[...] = jnp.zeros_like(ys_ref)


def _experts(tile_expert, tile_on, last_tile, xs, w_gate_up, b_gate_up, w_down, b_down):
    R, D = xs.shape
    TM = EXPERT_TILE
    row = lambda i, te, on, last: (jnp.minimum(i, last[0]), 0)
    per_e = lambda i, te, on, last: (te[i], 0, 0)
    grid_spec = pltpu.PrefetchScalarGridSpec(
        num_scalar_prefetch=3,
        grid=(R // TM,),
        in_specs=[
            pl.BlockSpec((TM, D), row),
            pl.BlockSpec((1, D, 2 * D_FF), per_e),
            pl.BlockSpec((1, 1, 2 * D_FF), per_e),
            pl.BlockSpec((1, D_FF, D), per_e),
            pl.BlockSpec((1, 1, D), per_e),
        ],
        out_specs=pl.BlockSpec((TM, D), lambda i, te, on, last: (i, 0)),
        scratch_shapes=[pltpu.VMEM((D, 2 * D_FF), BF16), pltpu.VMEM((D_FF, D), BF16)],
    )
    return pl.pallas_call(
        _expert_kernel,
        grid_spec=grid_spec,
        out_shape=jax.ShapeDtypeStruct((R, D), F32),
        compiler_params=pltpu.CompilerParams(
            dimension_semantics=("arbitrary",), vmem_limit_bytes=VMEM_LIMIT),
        name="experts",
    )(tile_expert, tile_on, last_tile, xs, w_gate_up,
      b_gate_up.reshape(N_EXPERTS, 1, 2 * D_FF), w_down, b_down.reshape(N_EXPERTS, 1, D))


def _combine_kernel(T, pos_ref, meta_ref, h_ref, g_ref, ys_ref, out_ref, gbuf, sem):
    def start(t, carry):
        for k in range(TOP_K):
            _row_copy(ys_ref, pos_ref[TOP_K * t + k], gbuf.at[k], t, sem).start(priority=k % 2)
        return carry

    def wait(t, carry):
        for k in range(TOP_K):
            _row_copy(ys_ref, 0, gbuf.at[k], 0, sem).wait()
        return carry

    lax.fori_loop(0, T, start, 0, unroll=8)
    lax.fori_loop(0, T, wait, 0, unroll=8)
    y = h_ref[...]
    for k in range(TOP_K):
        wk = meta_ref[:, META_WEIGHT + k:META_WEIGHT + k + 1]
        y = y + wk * gbuf[k]
    out_ref[...] = y * lax.rsqrt(jnp.mean(y * y, axis=-1, keepdims=True) + NORM_EPS) * g_ref[...]


def _combine(pos_flat, meta, h, final_g, ys):
    N, D = h.shape
    T = TOKEN_TILE
    return pl.pallas_call(
        functools.partial(_combine_kernel, T),
        grid=(N // T,),
        in_specs=[
            pl.BlockSpec((T * TOP_K,), lambda i: (i,), memory_space=pltpu.SMEM),
            pl.BlockSpec((T, META_LANES), lambda i: (i, 0)),
            pl.BlockSpec((T, D), lambda i: (i, 0)),
            pl.BlockSpec((1, D), lambda i: (0, 0)),
            pl.BlockSpec(memory_space=pl.ANY),
        ],
        out_specs=pl.BlockSpec((T, D), lambda i: (i, 0)),
        out_shape=jax.ShapeDtypeStruct((N, D), F32),
        scratch_shapes=[pltpu.VMEM((TOP_K, T, D), F32), pltpu.SemaphoreType.DMA(())],
        compiler_params=pltpu.CompilerParams(
            dimension_semantics=("arbitrary",), vmem_limit_bytes=VMEM_LIMIT),
        name="combine",
    )(pos_flat, meta, h, final_g.reshape(1, D), ys)


def _sorted_positions(meta, group_start):
    ids = meta[:, 0:TOP_K].astype(I32)
    rank = meta[:, META_RANK:META_RANK + TOP_K].astype(I32)
    experts = jnp.arange(N_EXPERTS, dtype=I32)
    start = jnp.sum(jnp.where(ids[:, :, None] == experts, group_start, 0), axis=-1)
    return (start + rank).reshape(-1)


def kernel(x_prompt, x_sample, state_pool, state_conv, norm1_g, w_in, w_pool_grp, pool_scale, w_pool_out, dw_w, dw_b, conv_ln_g, conv_ln_b, w_conv_out, w_out, norm2_g, w_router, b_router, w_gate_up, b_gate_up, w_down, b_down, final_g):
    depth = norm1_g.shape[0]
    assert depth == 1
    B, L, D = x_prompt.shape
    SB, SL, _ = x_sample.shape
    row = lambda v: v[0].reshape(1, -1)
    weights = (
        row(norm1_g), w_in[0].astype(BF16), w_pool_grp[0].astype(BF16), row(pool_scale),
        w_pool_out[0].astype(BF16), dw_w[0], row(dw_b), row(conv_ln_g), row(conv_ln_b),
        w_conv_out[0].astype(BF16), w_out[0].astype(BF16), row(norm2_g),
        w_router[0].astype(BF16), row(b_router),
    )
    zero_pool = jnp.zeros((B, POOL_HIST, POOL_WIDTH), F32)
    zero_conv = jnp.zeros((B, CONV_HIST, CONV_DIM), F32)
    zero_cnt = jnp.zeros((1, N_EXPERTS), F32)

    h_p, xn_p, meta_p, pool_p, conv_p, cnt_p = _front(
        x_prompt, zero_pool, zero_conv, zero_cnt, weights, 1, FRONT_TILE, 0)
    h_s, xn_s, meta_s, pool_s, conv_s, cnt_s = _front(
        x_sample, state_pool[0], state_conv[0], cnt_p, weights, SAMPLE_STREAMS, SL, L)

    n_tok = B * L + SB * SL
    n_tiles = (n_tok * TOP_K + N_EXPERTS * (EXPERT_TILE - 1)) // EXPERT_TILE
    counts = cnt_s[0].astype(I32)
    tiles_per_e = (counts + EXPERT_TILE - 1) // EXPERT_TILE
    tile_end = jnp.cumsum(tiles_per_e)
    group_start = (tile_end - tiles_per_e) * EXPERT_TILE
    tile_ids = jnp.arange(n_tiles, dtype=I32)
    tile_expert = jnp.minimum(
        jnp.sum((tile_ids[:, None] >= tile_end[None, :]).astype(I32), axis=1), N_EXPERTS - 1)
    tile_on = (tile_ids < tile_end[-1]).astype(I32)
    last_tile = (tile_end[-1:] - 1).astype(I32)

    meta_p2 = meta_p.reshape(B * L, META_LANES)
    meta_s2 = meta_s.reshape(SB * SL, META_LANES)
    pos_p = _sorted_positions(meta_p2, group_start)
    pos_s = _sorted_positions(meta_s2, group_start)

    zero_rows = jnp.where(tiles_per_e > 0, (tile_end - 1) * EXPERT_TILE, -1).astype(I32)
    xs = _dispatch(jnp.concatenate([pos_p, pos_s]), zero_rows, tile_end[-1:].astype(I32),
                   xn_p.reshape(B * L, D), xn_s.reshape(SB * SL, D), n_tiles)
    ys = _experts(tile_expert, tile_on, last_tile, xs, w_gate_up[0], b_gate_up[0], w_down[0], b_down[0])
    y_p = _combine(pos_p, meta_p2, h_p.reshape(B * L, D), final_g, ys)
    y_s = _combine(pos_s, meta_s2, h_s.reshape(SB * SL, D), final_g, ys)
    return (y_p.reshape(B, L, D), y_s.reshape(SB, SL, D), pool_p, conv_p, pool_s, conv_s)
```

```python
import functools

import jax
import jax.numpy as jnp
from jax import lax
from jax.experimental import pallas as pl
from jax.experimental.pallas import tpu as pltpu

F32 = jnp.float32
BF16 = jnp.bfloat16
I32 = jnp.int32

D_MODEL = 1024
POOL_WINDOWS = (2, 4, 8, 16)
POOL_WIDTH = 512
POOL_GROUP_DIM = 128
POOL_HIST = 15
CONV_DIM = 1024
CONV_WIDTH = 31
CONV_HIST = 30
N_EXPERTS = 32
TOP_K = 4
D_FF = 1024
SWIGLU_LIMIT = 7.0
SWIGLU_ALPHA = 1.702
NORM_EPS = 1e-6

SUBLANES = 8
POOL_BASE = 16
CONV_BASE = 32
META_LANES = 128
META_RANK = 4
META_WEIGHT = 8

FRONT_TILE = 256
SAMPLE_STREAMS = 8
TOKEN_TILE = 256
EXPERT_TILE = 256
CONV_CHUNK_ROWS = 32
VMEM_LIMIT = 56 * 1024 * 1024


def _sigmoid(v):
    return 1.0 / (1.0 + jnp.exp(-v))


def _dot(a, b):
    return jnp.dot(a, b, preferred_element_type=F32)


def _front_kernel(S, Lt, pos0,
                  x_ref, hp_ref, hc_ref, base_ref, g1_ref, win_ref, wg_ref, ps_ref, wpo_ref,
                  dww_ref, dwb_ref, lng_ref, lnb_ref, wco_ref, wo_ref, g2_ref, wr_ref, br_ref,
                  h_ref, xn_ref, meta_ref, np_ref, nc_ref, cnt_ref,
                  ep_ref, ec_ref, sb_ref, run_ref, band_ref):
    b = pl.program_id(0)
    l = pl.program_id(1)
    T = S * Lt
    P, C, D = POOL_WIDTH, CONV_DIM, D_MODEL
    tap0 = CONV_BASE - CONV_HIST
    first = l == 0

    @pl.when((b == 0) & first)
    def _():
        run_ref[...] = base_ref[...]
        col = lax.broadcasted_iota(I32, (Lt, POOL_BASE + Lt), 1)
        newest = lax.broadcasted_iota(I32, (Lt, POOL_BASE + Lt), 0) + POOL_BASE
        for g, w in enumerate(POOL_WINDOWS):
            band_ref[g] = ((col <= newest) & (col > newest - w)).astype(BF16)
        ep_ref[...] = jnp.zeros_like(ep_ref)
        ec_ref[...] = jnp.zeros_like(ec_ref)

    def conv_half():
        rc = min(CONV_CHUNK_ROWS, Lt)
        sc = CONV_CHUNK_ROWS // rc
        for s0 in range(0, S, sc):
            for r0 in range(0, Lt, rc):
                acc = None
                for j in range(CONV_WIDTH):
                    r = (tap0 + j) % SUBLANES
                    base = tap0 + j - r + r0
                    rows = ec_ref[r, s0:s0 + sc, base:base + rc, :]
                    term = rows.reshape(sc * rc // SUBLANES, SUBLANES, C) * dww_ref[j]
                    acc = term if acc is None else acc + term
                v = acc + dwb_ref[...]
                mu = jnp.mean(v, axis=-1, keepdims=True)
                vc = v - mu
                var = jnp.mean(vc * vc, axis=-1, keepdims=True)
                vn = vc * lax.rsqrt(var + NORM_EPS) * lng_ref[...] + lnb_ref[...]
                act = vn * _sigmoid(vn)
                row = s0 * Lt + r0
                sb_ref[row:row + sc * rc, :] = act.reshape(sc * rc, C).astype(BF16)

    def projection_half():
        ep_ref[:, POOL_BASE - POOL_HIST:POOL_BASE, :] = jnp.where(
            first, hp_ref[...], ep_ref[:, Lt + POOL_BASE - POOL_HIST:Lt + POOL_BASE, :])
        for r in range(SUBLANES):
            lo = max(0, tap0 - r)
            ec_ref[r, :, lo:CONV_BASE - r, :] = jnp.where(
                first, hc_ref[:, lo + r - tap0:CONV_HIST, :],
                ec_ref[r, :, Lt + lo:Lt + CONV_BASE - r, :])

        x = x_ref[...].reshape(T, D)
        n1 = x * lax.rsqrt(jnp.mean(x * x, axis=-1, keepdims=True) + NORM_EPS) * g1_ref[...]
        n1b = n1.astype(BF16)

        zp = _dot(n1b, win_ref[:, 0:P])
        ep_ref[:, POOL_BASE:POOL_BASE + Lt, :] = zp.reshape(S, Lt, P)
        np_ref[0] = ep_ref[:, Lt + POOL_BASE - POOL_HIST:Lt + POOL_BASE, :]
        pos = (pos0 + l * Lt + lax.broadcasted_iota(I32, (Lt, 1), 0)).astype(F32)
        ds = [[] for _ in POOL_WINDOWS]
        for s in range(S):
            ext = ep_ref[s]
            hi = ext.astype(BF16)
            lo_part = (ext - hi.astype(F32)).astype(BF16)
            for g, w in enumerate(POOL_WINDOWS):
                sl = slice(g * POOL_GROUP_DIM, (g + 1) * POOL_GROUP_DIM)
                parts = _dot(band_ref[g], jnp.concatenate([hi[:, sl], lo_part[:, sl]], axis=-1))
                total = parts[:, :POOL_GROUP_DIM] + parts[:, POOL_GROUP_DIM:]
                cnt = jnp.minimum(pos + 1.0, float(w))
                ds[g].append(total / cnt - ext[POOL_BASE:, sl])
        ys = []
        for g in range(len(POOL_WINDOWS)):
            d = ds[g][0] if S == 1 else jnp.concatenate(ds[g], axis=0)
            ys.append(_dot(d.astype(BF16), wg_ref[g]))
        yp = jnp.concatenate(ys, axis=-1) * ps_ref[...]
        a = _dot(yp.astype(BF16), wpo_ref[...])

        za = _dot(n1b, win_ref[:, P:P + C])
        zg = _dot(n1b, win_ref[:, P + C:P + 2 * C])
        u3 = (za * _sigmoid(zg)).reshape(S, Lt, C)
        for r in range(SUBLANES):
            ec_ref[r, :, CONV_BASE - r:CONV_BASE - r + Lt, :] = u3
        nc_ref[0] = ec_ref[tap0, :, Lt:Lt + CONV_HIST, :]

        ga = _sigmoid(_dot(n1b, win_ref[:, P + 2 * C:P + 2 * C + D]))
        gb = _sigmoid(_dot(n1b, win_ref[:, P + 2 * C + D:P + 2 * C + 2 * D]))
        return x, ga * a, gb

    x, gated_pool, gate_conv = projection_half()
    conv_half()
    _front_tail(S, Lt, x, gated_pool, gate_conv, sb_ref, wco_ref, wo_ref, g2_ref, wr_ref, br_ref,
                run_ref, h_ref, xn_ref, meta_ref, cnt_ref)


def _front_tail(S, Lt, x, gated_pool, gate_conv, sb_ref, wco_ref, wo_ref, g2_ref, wr_ref, br_ref,
                run_ref, h_ref, xn_ref, meta_ref, cnt_ref):
    T = S * Lt
    D = D_MODEL
    bb = _dot(sb_ref[...], wco_ref[...])
    m = gated_pool + gate_conv * bb
    h = x + _dot(m.astype(BF16), wo_ref[...])
    h_ref[...] = h.reshape(S, Lt, D)
    xn = h * lax.rsqrt(jnp.mean(h * h, axis=-1, keepdims=True) + NORM_EPS) * g2_ref[...]
    xn_ref[...] = xn.reshape(S, Lt, D)

    logits = _dot(xn.astype(BF16), wr_ref[...]) + br_ref[...]
    lane_e = lax.broadcasted_iota(I32, (T, N_EXPERTS), 1).astype(F32)
    work = logits
    sels, tops, ids = [], [], []
    for _ in range(TOP_K):
        mx = jnp.max(work, axis=-1, keepdims=True)
        idx = jnp.min(jnp.where(work == mx, lane_e, float(N_EXPERTS)), axis=-1, keepdims=True)
        sel = lane_e == idx
        work = jnp.where(sel, -jnp.inf, work)
        sels.append(sel)
        tops.append(mx)
        ids.append(idx)
    exps = [jnp.exp(t - tops[0]) for t in tops]
    denom = exps[0] + exps[1] + exps[2] + exps[3]
    wts = [e / denom for e in exps]

    mask = jnp.zeros((T, N_EXPERTS), F32)
    for sel in sels:
        mask = mask + sel.astype(F32)
    tri = (lax.broadcasted_iota(I32, (T, T), 1) < lax.broadcasted_iota(I32, (T, T), 0)).astype(BF16)
    excl = _dot(tri, mask.astype(BF16)) + run_ref[...]
    run_new = run_ref[...] + jnp.sum(mask, axis=0, keepdims=True)
    run_ref[...] = run_new
    cnt_ref[...] = run_new

    lane_m = lax.broadcasted_iota(I32, (T, META_LANES), 1)
    meta = jnp.zeros((T, META_LANES), F32)
    for k in range(TOP_K):
        rank = jnp.sum(jnp.where(sels[k], excl, 0.0), axis=-1, keepdims=True)
        meta = jnp.where(lane_m == k, ids[k], meta)
        meta = jnp.where(lane_m == META_RANK + k, rank, meta)
        meta = jnp.where(lane_m == META_WEIGHT + k, wts[k], meta)
    meta_ref[...] = meta.reshape(S, Lt, META_LANES)


def _const_spec(shape):
    nd = len(shape)
    return pl.BlockSpec(shape, lambda b, l: (0,) * nd, pipeline_mode=pl.Buffered(1))


def _front(x, hist_pool, hist_conv, base_cnt, weights, S, Lt, pos0):
    B, L, D = x.shape
    P, C = POOL_WIDTH, CONV_DIM
    grid = (B // S, L // Lt)
    T = S * Lt
    tok = lambda b, l: (b, l, 0)
    per_b = lambda b, l: (b, 0, 0)
    in_specs = [
        pl.BlockSpec((S, Lt, D), tok),
        pl.BlockSpec((S, POOL_HIST, P), per_b),
        pl.BlockSpec((S, CONV_HIST, C), per_b),
        _const_spec((1, N_EXPERTS)),
    ] + [_const_spec(w.shape) for w in weights]
    out_shape = (
        jax.ShapeDtypeStruct((B, L, D), F32),
        jax.ShapeDtypeStruct((B, L, D), F32),
        jax.ShapeDtypeStruct((B, L, META_LANES), F32),
        jax.ShapeDtypeStruct((1, B, POOL_HIST, P), F32),
        jax.ShapeDtypeStruct((1, B, CONV_HIST, C), F32),
        jax.ShapeDtypeStruct((1, N_EXPERTS), F32),
    )
    out_specs = (
        pl.BlockSpec((S, Lt, D), tok),
        pl.BlockSpec((S, Lt, D), tok),
        pl.BlockSpec((S, Lt, META_LANES), tok),
        pl.BlockSpec((1, S, POOL_HIST, P), lambda b, l: (0, b, 0, 0)),
        pl.BlockSpec((1, S, CONV_HIST, C), lambda b, l: (0, b, 0, 0)),
        pl.BlockSpec((1, N_EXPERTS), lambda b, l: (0, 0)),
    )
    scratch = [
        pltpu.VMEM((S, POOL_BASE + Lt, P), F32),
        pltpu.VMEM((SUBLANES, S, CONV_BASE + Lt, C), F32),
        pltpu.VMEM((T, C), BF16),
        pltpu.VMEM((1, N_EXPERTS), F32),
        pltpu.VMEM((len(POOL_WINDOWS), Lt, POOL_BASE + Lt), BF16),
    ]
    return pl.pallas_call(
        functools.partial(_front_kernel, S, Lt, pos0),
        grid=grid, in_specs=in_specs, out_specs=out_specs, out_shape=out_shape,
        scratch_shapes=scratch,
        compiler_params=pltpu.CompilerParams(
            dimension_semantics=("arbitrary", "arbitrary"), vmem_limit_bytes=VMEM_LIMIT),
        name="front",
    )(x, hist_pool, hist_conv, base_cnt, *weights)


def _row_copy(src_ref, src_row, dst_ref, dst_row, sem):
    return pltpu.make_async_copy(src_ref.at[pl.ds(src_row, 1)], dst_ref.at[pl.ds(dst_row, 1)], sem)


def _dispatch_kernel(T, steps_a, n_tiles, pos_ref, zrow_ref, used_ref, xa_ref, xb_ref, xs_ref,
                     sem, zbuf, zsem):
    i = pl.program_id(0)

    @pl.when(i == 0)
    def _():
        zbuf[...] = jnp.zeros_like(zbuf)

        def fill(row0):
            row0 = pl.multiple_of(row0, EXPERT_TILE)
            return pltpu.make_async_copy(zbuf, xs_ref.at[pl.ds(row0, EXPERT_TILE)], zsem)

        def tail_start(j, carry):
            fill(j * EXPERT_TILE).start()
            return carry

        def tail_wait(j, carry):
            fill(j * EXPERT_TILE).wait()
            return carry

        for e in range(N_EXPERTS):
            pl.when(zrow_ref[e] >= 0)(lambda e=e: fill(zrow_ref[e]).start())
        lax.fori_loop(used_ref[0], n_tiles, tail_start, 0)
        for e in range(N_EXPERTS):
            pl.when(zrow_ref[e] >= 0)(lambda e=e: fill(zrow_ref[e]).wait())
        lax.fori_loop(used_ref[0], n_tiles, tail_wait, 0)

    def scatter(src_ref):
        def start(t, carry):
            for k in range(TOP_K):
                _row_copy(src_ref, t, xs_ref, pos_ref[TOP_K * t + k], sem).start(priority=k % 2)
            return carry

        def wait(t, carry):
            for k in range(TOP_K):
                _row_copy(src_ref, 0, xs_ref, 0, sem).wait()
            return carry

        lax.fori_loop(0, T, start, 0, unroll=8)
        lax.fori_loop(0, T, wait, 0, unroll=8)

    pl.when(i < steps_a)(lambda: scatter(xa_ref))
    pl.when(i >= steps_a)(lambda: scatter(xb_ref))


def _dispatch(pos_flat, zero_rows, used_tiles, xn_a, xn_b, n_tiles):
    D = xn_a.shape[1]
    T = TOKEN_TILE
    steps_a, steps_b = xn_a.shape[0] // T, xn_b.shape[0] // T
    return pl.pallas_call(
        functools.partial(_dispatch_kernel, T, steps_a, n_tiles),
        grid=(steps_a + steps_b,),
        in_specs=[
            pl.BlockSpec((T * TOP_K,), lambda i: (i,), memory_space=pltpu.SMEM),
            pl.BlockSpec(memory_space=pltpu.SMEM),
            pl.BlockSpec(memory_space=pltpu.SMEM),
            pl.BlockSpec((T, D), lambda i: (jnp.minimum(i, steps_a - 1), 0)),
            pl.BlockSpec((T, D), lambda i: (jnp.maximum(i - steps_a, 0), 0)),
        ],
        out_specs=pl.BlockSpec(memory_space=pl.ANY),
        out_shape=jax.ShapeDtypeStruct((n_tiles * EXPERT_TILE, D), F32),
        scratch_shapes=[pltpu.SemaphoreType.DMA(()), pltpu.VMEM((EXPERT_TILE, D), F32),
                        pltpu.SemaphoreType.DMA(())],
        compiler_params=pltpu.CompilerParams(dimension_semantics=("arbitrary",)),
        name="dispatch",
    )(pos_flat, zero_rows, used_tiles, xn_a, xn_b)


def _expert_kernel(te_ref, on_ref, last_ref, next_ref, slot_ref,
                   xs_ref, wgu_hbm, bgu_ref, wd_hbm, bd_ref, ys_ref,
                   wgu_f32, wd_f32, wgu_bf, wd_bf, sem_gu, sem_d):
    del last_ref
    i = pl.program_id(0)
    e = te_ref[i]
    slot = slot_ref[i]
    group_start = (i == 0) | (e != te_ref[jnp.maximum(i - 1, 0)])

    def weight_copies(expert, s):
        return (pltpu.make_async_copy(wgu_hbm.at[expert], wgu_f32.at[s], sem_gu.at[s]),
                pltpu.make_async_copy(wd_hbm.at[expert], wd_f32.at[s], sem_d.at[s]))

    @pl.when(i == 0)
    def _():
        for c in weight_copies(e, slot):
            c.start()

    @pl.when(group_start & (on_ref[i] == 1))
    def _():
        for c in weight_copies(e, slot):
            c.wait()

        @pl.when(next_ref[i] >= 0)
        def _():
            for c in weight_copies(next_ref[i], 1 - slot):
                c.start()

        wgu_bf[...] = wgu_f32[slot].astype(BF16)
        wd_bf[...] = wd_f32[slot].astype(BF16)

    @pl.when(on_ref[i] == 1)
    def _():
        xb = xs_ref[...].astype(BF16)
        gu = _dot(xb, wgu_bf[...]) + bgu_ref[0]
        gate = jnp.minimum(gu[:, :D_FF], SWIGLU_LIMIT)
        up = jnp.clip(gu[:, D_FF:], -SWIGLU_LIMIT, SWIGLU_LIMIT)
        hm = (up + 1.0) * (gate * _sigmoid(SWIGLU_ALPHA * gate))
        ys_ref[...] = _dot(hm.astype(BF16), wd_bf[...]) + bd_ref[0]

    @pl.when(on_ref[i] == 0)
    def _():
        ys_ref[...] = jnp.zeros_like(ys_ref)


def _experts(tile_expert, tile_on, last_tile, tile_next, tile_slot, xs, w_gate_up, b_gate_up,
             w_down, b_down):
    R, D = xs.shape
    TM = EXPERT_TILE
    row = lambda i, te, on, last, nxt, slot: (jnp.minimum(i, last[0]), 0)
    per_e = lambda i, te, on, last, nxt, slot: (te[i], 0, 0)
    grid_spec = pltpu.PrefetchScalarGridSpec(
        num_scalar_prefetch=5,
        grid=(R // TM,),
        in_specs=[
            pl.BlockSpec((TM, D), row),
            pl.BlockSpec(memory_space=pl.ANY),
            pl.BlockSpec((1, 1, 2 * D_FF), per_e),
            pl.BlockSpec(memory_space=pl.ANY),
            pl.BlockSpec((1, 1, D), per_e),
        ],
        out_specs=pl.BlockSpec((TM, D), lambda i, te, on, last, nxt, slot: (i, 0)),
        scratch_shapes=[
            pltpu.VMEM((2, D, 2 * D_FF), F32), pltpu.VMEM((2, D_FF, D), F32),
            pltpu.VMEM((D, 2 * D_FF), BF16), pltpu.VMEM((D_FF, D), BF16),
            pltpu.SemaphoreType.DMA((2,)), pltpu.SemaphoreType.DMA((2,)),
        ],
    )
    return pl.pallas_call(
        _expert_kernel,
        grid_spec=grid_spec,
        out_shape=jax.ShapeDtypeStruct((R, D), F32),
        compiler_params=pltpu.CompilerParams(
            dimension_semantics=("arbitrary",), vmem_limit_bytes=VMEM_LIMIT),
        name="experts",
    )(tile_expert, tile_on, last_tile, tile_next, tile_slot, xs, w_gate_up,
      b_gate_up.reshape(N_EXPERTS, 1, 2 * D_FF), w_down, b_down.reshape(N_EXPERTS, 1, D))


def _combine_kernel(T, pos_ref, meta_ref, h_ref, g_ref, ys_ref, out_ref, gbuf, sem):
    def start(t, carry):
        for k in range(TOP_K):
            _row_copy(ys_ref, pos_ref[TOP_K * t + k], gbuf.at[k], t, sem).start(priority=k % 2)
        return carry

    def wait(t, carry):
        for k in range(TOP_K):
            _row_copy(ys_ref, 0, gbuf.at[k], 0, sem).wait()
        return carry

    lax.fori_loop(0, T, start, 0, unroll=8)
    lax.fori_loop(0, T, wait, 0, unroll=8)
    y = h_ref[...]
    for k in range(TOP_K):
        wk = meta_ref[:, META_WEIGHT + k:META_WEIGHT + k + 1]
        y = y + wk * gbuf[k]
    out_ref[...] = y * lax.rsqrt(jnp.mean(y * y, axis=-1, keepdims=True) + NORM_EPS) * g_ref[...]


def _combine(pos_flat, meta, h, final_g, ys):
    N, D = h.shape
    T = TOKEN_TILE
    return pl.pallas_call(
        functools.partial(_combine_kernel, T),
        grid=(N // T,),
        in_specs=[
            pl.BlockSpec((T * TOP_K,), lambda i: (i,), memory_space=pltpu.SMEM),
            pl.BlockSpec((T, META_LANES), lambda i: (i, 0)),
            pl.BlockSpec((T, D), lambda i: (i, 0)),
            pl.BlockSpec((1, D), lambda i: (0, 0)),
            pl.BlockSpec(memory_space=pl.ANY),
        ],
        out_specs=pl.BlockSpec((T, D), lambda i: (i, 0)),
        out_shape=jax.ShapeDtypeStruct((N, D), F32),
        scratch_shapes=[pltpu.VMEM((TOP_K, T, D), F32), pltpu.SemaphoreType.DMA(())],
        compiler_params=pltpu.CompilerParams(
            dimension_semantics=("arbitrary",), vmem_limit_bytes=VMEM_LIMIT),
        name="combine",
    )(pos_flat, meta, h, final_g.reshape(1, D), ys)


def _sorted_positions(meta, group_start):
    ids = meta[:, 0:TOP_K].astype(I32)
    rank = meta[:, META_RANK:META_RANK + TOP_K].astype(I32)
    experts = jnp.arange(N_EXPERTS, dtype=I32)
    start = jnp.sum(jnp.where(ids[:, :, None] == experts, group_start, 0), axis=-1)
    return (start + rank).reshape(-1)


def kernel(x_prompt, x_sample, state_pool, state_conv, norm1_g, w_in, w_pool_grp, pool_scale, w_pool_out, dw_w, dw_b, conv_ln_g, conv_ln_b, w_conv_out, w_out, norm2_g, w_router, b_router, w_gate_up, b_gate_up, w_down, b_down, final_g):
    depth = norm1_g.shape[0]
    assert depth == 1
    B, L, D = x_prompt.shape
    SB, SL, _ = x_sample.shape
    row = lambda v: v[0].reshape(1, -1)
    weights = (
        row(norm1_g), w_in[0].astype(BF16), w_pool_grp[0].astype(BF16), row(pool_scale),
        w_pool_out[0].astype(BF16),
        jnp.broadcast_to(dw_w[0][:, None, :], (CONV_WIDTH, SUBLANES, CONV_DIM)),
        row(dw_b), row(conv_ln_g), row(conv_ln_b),
        w_conv_out[0].astype(BF16), w_out[0].astype(BF16), row(norm2_g),
        w_router[0].astype(BF16), row(b_router),
    )
    zero_pool = jnp.zeros((B, POOL_HIST, POOL_WIDTH), F32)
    zero_conv = jnp.zeros((B, CONV_HIST, CONV_DIM), F32)
    zero_cnt = jnp.zeros((1, N_EXPERTS), F32)

    h_p, xn_p, meta_p, pool_p, conv_p, cnt_p = _front(
        x_prompt, zero_pool, zero_conv, zero_cnt, weights, 1, FRONT_TILE, 0)
    h_s, xn_s, meta_s, pool_s, conv_s, cnt_s = _front(
        x_sample, state_pool[0], state_conv[0], cnt_p, weights, SAMPLE_STREAMS, SL, L)

    n_tok = B * L + SB * SL
    n_tiles = (n_tok * TOP_K + N_EXPERTS * (EXPERT_TILE - 1)) // EXPERT_TILE
    counts = cnt_s[0].astype(I32)
    tiles_per_e = (counts + EXPERT_TILE - 1) // EXPERT_TILE
    tile_end = jnp.cumsum(tiles_per_e)
    group_start = (tile_end - tiles_per_e) * EXPERT_TILE
    tile_ids = jnp.arange(n_tiles, dtype=I32)
    tile_expert = jnp.minimum(
        jnp.sum((tile_ids[:, None] >= tile_end[None, :]).astype(I32), axis=1), N_EXPERTS - 1)
    tile_on = (tile_ids < tile_end[-1]).astype(I32)
    last_tile = (tile_end[-1:] - 1).astype(I32)
    expert_ids = jnp.arange(N_EXPERTS, dtype=I32)
    has_rows = tiles_per_e > 0
    later = (expert_ids[None, :] > expert_ids[:, None]) & has_rows[None, :]
    next_e = jnp.min(jnp.where(later, expert_ids[None, :], N_EXPERTS), axis=1)
    next_e = jnp.where(next_e < N_EXPERTS, next_e, -1).astype(I32)
    slot_e = ((jnp.cumsum(has_rows.astype(I32)) - 1) % 2).astype(I32)
    tile_next = next_e[tile_expert]
    tile_slot = jnp.maximum(slot_e[tile_expert], 0)

    meta_p2 = meta_p.reshape(B * L, META_LANES)
    meta_s2 = meta_s.reshape(SB * SL, META_LANES)
    pos_p = _sorted_positions(meta_p2, group_start)
    pos_s = _sorted_positions(meta_s2, group_start)

    zero_rows = jnp.where(tiles_per_e > 0, (tile_end - 1) * EXPERT_TILE, -1).astype(I32)
    xs = _dispatch(jnp.concatenate([pos_p, pos_s]), zero_rows, tile_end[-1:].astype(I32),
                   xn_p.reshape(B * L, D), xn_s.reshape(SB * SL, D), n_tiles)
    ys = _experts(tile_expert, tile_on, last_tile, tile_next, tile_slot, xs,
                  w_gate_up[0], b_gate_up[0], w_down[0], b_down[0])
    y_p = _combine(pos_p, meta_p2, h_p.reshape(B * L, D), final_g, ys)
    y_s = _combine(pos_s, meta_s2, h_s.reshape(SB * SL, D), final_g, ys)
    return (y_p.reshape(B, L, D), y_s.reshape(SB, SL, D), pool_p, conv_p, pool_s, conv_s)
```

```python
import functools

import jax
import jax.numpy as jnp
from jax import lax
from jax.experimental import pallas as pl
from jax.experimental.pallas import tpu as pltpu

F32 = jnp.float32
BF16 = jnp.bfloat16
I32 = jnp.int32

D_MODEL = 1024
POOL_WINDOWS = (2, 4, 8, 16)
POOL_WIDTH = 512
POOL_GROUP_DIM = 128
POOL_HIST = 15
CONV_DIM = 1024
CONV_WIDTH = 31
CONV_HIST = 30
N_EXPERTS = 32
TOP_K = 4
D_FF = 1024
SWIGLU_LIMIT = 7.0
SWIGLU_ALPHA = 1.702
NORM_EPS = 1e-6

SUBLANES = 8
LANES = 128
POOL_BASE = 16
CONV_BASE = 32
META_LANES = 128
META_RANK = 4
META_WEIGHT = 8

FRONT_TILE = 256
SAMPLE_STREAMS = 8
TOKEN_TILE = 256
EXPERT_TILE = 256
CONV_CHUNK_ROWS = 32
VMEM_LIMIT = 56 * 1024 * 1024


def _sigmoid(v):
    return 1.0 / (1.0 + jnp.exp(-v))


def _dot(a, b):
    return jnp.dot(a, b, preferred_element_type=F32)


def _front_kernel(S, Lt, pos0,
                  x_ref, hp_ref, hc_ref, base_ref, g1_ref, win_ref, wg_ref, ps_ref, wpo_ref,
                  dww_ref, dwb_ref, lng_ref, lnb_ref, wco_ref, wo_ref, g2_ref, wr_ref, br_ref,
                  h_ref, xn_ref, meta_ref, np_ref, nc_ref, cnt_ref,
                  ep_ref, ec_ref, sb_ref, run_ref, band_ref):
    b = pl.program_id(0)
    l = pl.program_id(1)
    T = S * Lt
    P, C, D = POOL_WIDTH, CONV_DIM, D_MODEL
    tap0 = CONV_BASE - CONV_HIST
    first = l == 0

    @pl.when((b == 0) & first)
    def _():
        run_ref[...] = base_ref[...]
        col = lax.broadcasted_iota(I32, (Lt, POOL_BASE + Lt), 1)
        newest = lax.broadcasted_iota(I32, (Lt, POOL_BASE + Lt), 0) + POOL_BASE
        for g, w in enumerate(POOL_WINDOWS):
            band_ref[g] = ((col <= newest) & (col > newest - w)).astype(BF16)
        ep_ref[...] = jnp.zeros_like(ep_ref)
        ec_ref[...] = jnp.zeros_like(ec_ref)

    def conv_half():
        rc = min(CONV_CHUNK_ROWS, Lt)
        sc = CONV_CHUNK_ROWS // rc
        for s0 in range(0, S, sc):
            for r0 in range(0, Lt, rc):
                acc = None
                for j in range(CONV_WIDTH):
                    r = (tap0 + j) % SUBLANES
                    base = tap0 + j - r + r0
                    rows = ec_ref[r, s0:s0 + sc, base:base + rc, :]
                    term = rows.reshape(sc * rc // SUBLANES, SUBLANES, C) * dww_ref[j]
                    acc = term if acc is None else acc + term
                v = acc + dwb_ref[...]
                mu = jnp.mean(v, axis=-1, keepdims=True)
                vc = v - mu
                var = jnp.mean(vc * vc, axis=-1, keepdims=True)
                vn = vc * lax.rsqrt(var + NORM_EPS) * lng_ref[...] + lnb_ref[...]
                act = vn * _sigmoid(vn)
                row = s0 * Lt + r0
                sb_ref[row:row + sc * rc, :] = act.reshape(sc * rc, C).astype(BF16)

    def projection_half():
        ep_ref[:, POOL_BASE - POOL_HIST:POOL_BASE, :] = jnp.where(
            first, hp_ref[...], ep_ref[:, Lt + POOL_BASE - POOL_HIST:Lt + POOL_BASE, :])
        for r in range(SUBLANES):
            lo = max(0, tap0 - r)
            ec_ref[r, :, lo:CONV_BASE - r, :] = jnp.where(
                first, hc_ref[:, lo + r - tap0:CONV_HIST, :],
                ec_ref[r, :, Lt + lo:Lt + CONV_BASE - r, :])

        x = x_ref[...].reshape(T, D)
        n1 = x * lax.rsqrt(jnp.mean(x * x, axis=-1, keepdims=True) + NORM_EPS) * g1_ref[...]
        n1b = n1.astype(BF16)

        zp = _dot(n1b, win_ref[:, 0:P])
        ep_ref[:, POOL_BASE:POOL_BASE + Lt, :] = zp.reshape(S, Lt, P)
        np_ref[0] = ep_ref[:, Lt + POOL_BASE - POOL_HIST:Lt + POOL_BASE, :]
        pos = (pos0 + l * Lt + lax.broadcasted_iota(I32, (Lt, 1), 0)).astype(F32)
        ds = [[] for _ in POOL_WINDOWS]
        for s in range(S):
            ext = ep_ref[s]
            hi = ext.astype(BF16)
            lo_part = (ext - hi.astype(F32)).astype(BF16)
            for g, w in enumerate(POOL_WINDOWS):
                sl = slice(g * POOL_GROUP_DIM, (g + 1) * POOL_GROUP_DIM)
                parts = _dot(band_ref[g], jnp.concatenate([hi[:, sl], lo_part[:, sl]], axis=-1))
                total = parts[:, :POOL_GROUP_DIM] + parts[:, POOL_GROUP_DIM:]
                cnt = jnp.minimum(pos + 1.0, float(w))
                ds[g].append(total / cnt - ext[POOL_BASE:, sl])
        ys = []
        for g in range(len(POOL_WINDOWS)):
            d = ds[g][0] if S == 1 else jnp.concatenate(ds[g], axis=0)
            ys.append(_dot(d.astype(BF16), wg_ref[g]))
        yp = jnp.concatenate(ys, axis=-1) * ps_ref[...]
        a = _dot(yp.astype(BF16), wpo_ref[...])

        za = _dot(n1b, win_ref[:, P:P + C])
        zg = _dot(n1b, win_ref[:, P + C:P + 2 * C])
        u3 = (za * _sigmoid(zg)).reshape(S, Lt, C)
        for r in range(SUBLANES):
            ec_ref[r, :, CONV_BASE - r:CONV_BASE - r + Lt, :] = u3
        nc_ref[0] = ec_ref[tap0, :, Lt:Lt + CONV_HIST, :]

        ga = _sigmoid(_dot(n1b, win_ref[:, P + 2 * C:P + 2 * C + D]))
        gb = _sigmoid(_dot(n1b, win_ref[:, P + 2 * C + D:P + 2 * C + 2 * D]))
        return x, ga * a, gb

    x, gated_pool, gate_conv = projection_half()
    conv_half()
    _front_tail(S, Lt, x, gated_pool, gate_conv, sb_ref, wco_ref, wo_ref, g2_ref, wr_ref, br_ref,
                run_ref, h_ref, xn_ref, meta_ref, cnt_ref)


def _front_tail(S, Lt, x, gated_pool, gate_conv, sb_ref, wco_ref, wo_ref, g2_ref, wr_ref, br_ref,
                run_ref, h_ref, xn_ref, meta_ref, cnt_ref):
    T = S * Lt
    D = D_MODEL
    bb = _dot(sb_ref[...], wco_ref[...])
    m = gated_pool + gate_conv * bb
    h = x + _dot(m.astype(BF16), wo_ref[...])
    h_ref[...] = h.reshape(S, Lt, D)
    xn = h * lax.rsqrt(jnp.mean(h * h, axis=-1, keepdims=True) + NORM_EPS) * g2_ref[...]
    for s in range(SUBLANES):
        xn_ref[pl.ds(s, T, stride=SUBLANES), :] = xn[:, s * LANES:(s + 1) * LANES]

    logits = _dot(xn.astype(BF16), wr_ref[...]) + br_ref[...]
    lane_e = lax.broadcasted_iota(I32, (T, N_EXPERTS), 1).astype(F32)
    work = logits
    sels, tops, ids = [], [], []
    for _ in range(TOP_K):
        mx = jnp.max(work, axis=-1, keepdims=True)
        idx = jnp.min(jnp.where(work == mx, lane_e, float(N_EXPERTS)), axis=-1, keepdims=True)
        sel = lane_e == idx
        work = jnp.where(sel, -jnp.inf, work)
        sels.append(sel)
        tops.append(mx)
        ids.append(idx)
    exps = [jnp.exp(t - tops[0]) for t in tops]
    denom = exps[0] + exps[1] + exps[2] + exps[3]
    wts = [e / denom for e in exps]

    mask = jnp.zeros((T, N_EXPERTS), F32)
    for sel in sels:
        mask = mask + sel.astype(F32)
    tri = (lax.broadcasted_iota(I32, (T, T), 1) < lax.broadcasted_iota(I32, (T, T), 0)).astype(BF16)
    excl = _dot(tri, mask.astype(BF16)) + run_ref[...]
    run_new = run_ref[...] + jnp.sum(mask, axis=0, keepdims=True)
    run_ref[...] = run_new
    cnt_ref[...] = run_new

    lane_m = lax.broadcasted_iota(I32, (T, META_LANES), 1)
    meta = jnp.zeros((T, META_LANES), F32)
    for k in range(TOP_K):
        rank = jnp.sum(jnp.where(sels[k], excl, 0.0), axis=-1, keepdims=True)
        meta = jnp.where(lane_m == k, ids[k], meta)
        meta = jnp.where(lane_m == META_RANK + k, rank, meta)
        meta = jnp.where(lane_m == META_WEIGHT + k, wts[k], meta)
    meta_ref[...] = meta.reshape(S, Lt, META_LANES)


def _const_spec(shape):
    nd = len(shape)
    return pl.BlockSpec(shape, lambda b, l: (0,) * nd, pipeline_mode=pl.Buffered(1))


def _front(x, hist_pool, hist_conv, base_cnt, weights, S, Lt, pos0):
    B, L, D = x.shape
    P, C = POOL_WIDTH, CONV_DIM
    grid = (B // S, L // Lt)
    T = S * Lt
    tok = lambda b, l: (b, l, 0)
    per_b = lambda b, l: (b, 0, 0)
    in_specs = [
        pl.BlockSpec((S, Lt, D), tok),
        pl.BlockSpec((S, POOL_HIST, P), per_b),
        pl.BlockSpec((S, CONV_HIST, C), per_b),
        _const_spec((1, N_EXPERTS)),
    ] + [_const_spec(w.shape) for w in weights]
    out_shape = (
        jax.ShapeDtypeStruct((B, L, D), F32),
        jax.ShapeDtypeStruct((B * L * SUBLANES, LANES), F32),
        jax.ShapeDtypeStruct((B, L, META_LANES), F32),
        jax.ShapeDtypeStruct((1, B, POOL_HIST, P), F32),
        jax.ShapeDtypeStruct((1, B, CONV_HIST, C), F32),
        jax.ShapeDtypeStruct((1, N_EXPERTS), F32),
    )
    steps_l = L // Lt
    out_specs = (
        pl.BlockSpec((S, Lt, D), tok),
        pl.BlockSpec((T * SUBLANES, LANES), lambda b, l: (b * steps_l + l, 0)),
        pl.BlockSpec((S, Lt, META_LANES), tok),
        pl.BlockSpec((1, S, POOL_HIST, P), lambda b, l: (0, b, 0, 0)),
        pl.BlockSpec((1, S, CONV_HIST, C), lambda b, l: (0, b, 0, 0)),
        pl.BlockSpec((1, N_EXPERTS), lambda b, l: (0, 0)),
    )
    scratch = [
        pltpu.VMEM((S, POOL_BASE + Lt, P), F32),
        pltpu.VMEM((SUBLANES, S, CONV_BASE + Lt, C), F32),
        pltpu.VMEM((T, C), BF16),
        pltpu.VMEM((1, N_EXPERTS), F32),
        pltpu.VMEM((len(POOL_WINDOWS), Lt, POOL_BASE + Lt), BF16),
    ]
    return pl.pallas_call(
        functools.partial(_front_kernel, S, Lt, pos0),
        grid=grid, in_specs=in_specs, out_specs=out_specs, out_shape=out_shape,
        scratch_shapes=scratch,
        compiler_params=pltpu.CompilerParams(
            dimension_semantics=("arbitrary", "arbitrary"), vmem_limit_bytes=VMEM_LIMIT),
        name="front",
    )(x, hist_pool, hist_conv, base_cnt, *weights)


def _row_copy(src_ref, src_row, dst_ref, dst_row, sem):
    return pltpu.make_async_copy(src_ref.at[pl.ds(src_row, 1)], dst_ref.at[pl.ds(dst_row, 1)], sem)


def _dispatch_kernel(T, steps_a, n_tiles, pos_ref, zrow_ref, used_ref, xa_ref, xb_ref, xs_ref,
                     sem, zbuf, zsem):
    i = pl.program_id(0)

    @pl.when(i == 0)
    def _():
        zbuf[...] = jnp.zeros_like(zbuf)

        def fill(row0):
            row0 = pl.multiple_of(row0, EXPERT_TILE)
            return pltpu.make_async_copy(zbuf, xs_ref.at[pl.ds(row0, EXPERT_TILE)], zsem)

        def tail_start(j, carry):
            fill(j * EXPERT_TILE).start()
            return carry

        def tail_wait(j, carry):
            fill(j * EXPERT_TILE).wait()
            return carry

        for e in range(N_EXPERTS):
            pl.when(zrow_ref[e] >= 0)(lambda e=e: fill(zrow_ref[e]).start())
        lax.fori_loop(used_ref[0], n_tiles, tail_start, 0)
        for e in range(N_EXPERTS):
            pl.when(zrow_ref[e] >= 0)(lambda e=e: fill(zrow_ref[e]).wait())
        lax.fori_loop(used_ref[0], n_tiles, tail_wait, 0)

    def scatter(src_ref):
        def start(t, carry):
            for k in range(TOP_K):
                pltpu.make_async_copy(
                    src_ref.at[t], xs_ref.at[pos_ref[TOP_K * t + k]], sem).start(priority=k % 2)
            return carry

        def wait(t, carry):
            for k in range(TOP_K):
                pltpu.make_async_copy(src_ref.at[0], xs_ref.at[0], sem).wait()
            return carry

        lax.fori_loop(0, T, start, 0, unroll=8)
        lax.fori_loop(0, T, wait, 0, unroll=8)

    pl.when(i < steps_a)(lambda: scatter(xa_ref))
    pl.when(i >= steps_a)(lambda: scatter(xb_ref))


def _dispatch(pos_flat, zero_rows, used_tiles, xn_a, xn_b, n_tiles):
    T = TOKEN_TILE
    steps_a, steps_b = xn_a.shape[0] // T, xn_b.shape[0] // T
    tile = (SUBLANES, LANES)
    return pl.pallas_call(
        functools.partial(_dispatch_kernel, T, steps_a, n_tiles),
        grid=(steps_a + steps_b,),
        in_specs=[
            pl.BlockSpec((T * TOP_K,), lambda i: (i,), memory_space=pltpu.SMEM),
            pl.BlockSpec(memory_space=pltpu.SMEM),
            pl.BlockSpec(memory_space=pltpu.SMEM),
            pl.BlockSpec((T,) + tile, lambda i: (jnp.minimum(i, steps_a - 1), 0, 0)),
            pl.BlockSpec((T,) + tile, lambda i: (jnp.maximum(i - steps_a, 0), 0, 0)),
        ],
        out_specs=pl.BlockSpec(memory_space=pl.ANY),
        out_shape=jax.ShapeDtypeStruct((n_tiles * EXPERT_TILE,) + tile, F32),
        scratch_shapes=[pltpu.SemaphoreType.DMA(()), pltpu.VMEM((EXPERT_TILE,) + tile, F32),
                        pltpu.SemaphoreType.DMA(())],
        compiler_params=pltpu.CompilerParams(dimension_semantics=("arbitrary",)),
        name="dispatch",
    )(pos_flat, zero_rows, used_tiles, xn_a, xn_b)


def _expert_kernel(te_ref, on_ref, last_ref, next_ref, slot_ref,
                   xs_ref, wgu_hbm, bgu_ref, wd_hbm, bd_ref, ys_ref,
                   wgu_f32, wd_f32, wgu_bf, wd_bf, sem_gu, sem_d):
    del last_ref
    i = pl.program_id(0)
    e = te_ref[i]
    slot = slot_ref[i]
    group_start = (i == 0) | (e != te_ref[jnp.maximum(i - 1, 0)])

    def weight_copies(expert, s):
        return (pltpu.make_async_copy(wgu_hbm.at[expert], wgu_f32.at[s], sem_gu.at[s]),
                pltpu.make_async_copy(wd_hbm.at[expert], wd_f32.at[s], sem_d.at[s]))

    @pl.when(i == 0)
    def _():
        for c in weight_copies(e, slot):
            c.start()

    @pl.when(group_start & (on_ref[i] == 1))
    def _():
        for c in weight_copies(e, slot):
            c.wait()

        @pl.when(next_ref[i] >= 0)
        def _():
            for c in weight_copies(next_ref[i], 1 - slot):
                c.start()

        wgu_bf[...] = wgu_f32[slot].astype(BF16)
        wd_bf[...] = wd_f32[slot].astype(BF16)

    @pl.when(on_ref[i] == 1)
    def _():
        xb = jnp.concatenate(
            [xs_ref[pl.ds(s, EXPERT_TILE, stride=SUBLANES), :] for s in range(SUBLANES)],
            axis=-1).astype(BF16)
        gu = _dot(xb, wgu_bf[...]) + bgu_ref[0]
        gate = jnp.minimum(gu[:, :D_FF], SWIGLU_LIMIT)
        up = jnp.clip(gu[:, D_FF:], -SWIGLU_LIMIT, SWIGLU_LIMIT)
        hm = (up + 1.0) * (gate * _sigmoid(SWIGLU_ALPHA * gate))
        ys_ref[...] = _dot(hm.astype(BF16), wd_bf[...]) + bd_ref[0]

    @pl.when(on_ref[i] == 0)
    def _():
        ys_ref[...] = jnp.zeros_like(ys_ref)


def _experts(tile_expert, tile_on, last_tile, tile_next, tile_slot, xs, w_gate_up, b_gate_up,
             w_down, b_down):
    D = D_MODEL
    R = xs.shape[0] // SUBLANES
    TM = EXPERT_TILE
    row = lambda i, te, on, last, nxt, slot: (jnp.minimum(i, last[0]), 0)
    per_e = lambda i, te, on, last, nxt, slot: (te[i], 0, 0)
    grid_spec = pltpu.PrefetchScalarGridSpec(
        num_scalar_prefetch=5,
        grid=(R // TM,),
        in_specs=[
            pl.BlockSpec((TM * SUBLANES, LANES), row),
            pl.BlockSpec(memory_space=pl.ANY),
            pl.BlockSpec((1, 1, 2 * D_FF), per_e),
            pl.BlockSpec(memory_space=pl.ANY),
            pl.BlockSpec((1, 1, D), per_e),
        ],
        out_specs=pl.BlockSpec((TM, D), lambda i, te, on, last, nxt, slot: (i, 0)),
        scratch_shapes=[
            pltpu.VMEM((2, D, 2 * D_FF), F32), pltpu.VMEM((2, D_FF, D), F32),
            pltpu.VMEM((D, 2 * D_FF), BF16), pltpu.VMEM((D_FF, D), BF16),
            pltpu.SemaphoreType.DMA((2,)), pltpu.SemaphoreType.DMA((2,)),
        ],
    )
    return pl.pallas_call(
        _expert_kernel,
        grid_spec=grid_spec,
        out_shape=jax.ShapeDtypeStruct((R, D), F32),
        compiler_params=pltpu.CompilerParams(
            dimension_semantics=("arbitrary",), vmem_limit_bytes=VMEM_LIMIT),
        name="experts",
    )(tile_expert, tile_on, last_tile, tile_next, tile_slot, xs, w_gate_up,
      b_gate_up.reshape(N_EXPERTS, 1, 2 * D_FF), w_down, b_down.reshape(N_EXPERTS, 1, D))


def _combine_kernel(T, pos_ref, meta_ref, h_ref, g_ref, ys_ref, out_ref, gbuf, sem):
    def start(t, carry):
        for k in range(TOP_K):
            _row_copy(ys_ref, pos_ref[TOP_K * t + k], gbuf.at[k], t, sem).start(priority=k % 2)
        return carry

    def wait(t, carry):
        for k in range(TOP_K):
            _row_copy(ys_ref, 0, gbuf.at[k], 0, sem).wait()
        return carry

    lax.fori_loop(0, T, start, 0, unroll=8)
    lax.fori_loop(0, T, wait, 0, unroll=8)
    y = h_ref[...]
    for k in range(TOP_K):
        wk = meta_ref[:, META_WEIGHT + k:META_WEIGHT + k + 1]
        y = y + wk * gbuf[k]
    out_ref[...] = y * lax.rsqrt(jnp.mean(y * y, axis=-1, keepdims=True) + NORM_EPS) * g_ref[...]


def _combine(pos_flat, meta, h, final_g, ys):
    N, D = h.shape
    T = TOKEN_TILE
    return pl.pallas_call(
        functools.partial(_combine_kernel, T),
        grid=(N // T,),
        in_specs=[
            pl.BlockSpec((T * TOP_K,), lambda i: (i,), memory_space=pltpu.SMEM),
            pl.BlockSpec((T, META_LANES), lambda i: (i, 0)),
            pl.BlockSpec((T, D), lambda i: (i, 0)),
            pl.BlockSpec((1, D), lambda i: (0, 0)),
            pl.BlockSpec(memory_space=pl.ANY),
        ],
        out_specs=pl.BlockSpec((T, D), lambda i: (i, 0)),
        out_shape=jax.ShapeDtypeStruct((N, D), F32),
        scratch_shapes=[pltpu.VMEM((TOP_K, T, D), F32), pltpu.SemaphoreType.DMA(())],
        compiler_params=pltpu.CompilerParams(
            dimension_semantics=("arbitrary",), vmem_limit_bytes=VMEM_LIMIT),
        name="combine",
    )(pos_flat, meta, h, final_g.reshape(1, D), ys)


def _sorted_positions(meta, group_start):
    ids = meta[:, 0:TOP_K].astype(I32)
    rank = meta[:, META_RANK:META_RANK + TOP_K].astype(I32)
    experts = jnp.arange(N_EXPERTS, dtype=I32)
    start = jnp.sum(jnp.where(ids[:, :, None] == experts, group_start, 0), axis=-1)
    return (start + rank).reshape(-1)


def kernel(x_prompt, x_sample, state_pool, state_conv, norm1_g, w_in, w_pool_grp, pool_scale, w_pool_out, dw_w, dw_b, conv_ln_g, conv_ln_b, w_conv_out, w_out, norm2_g, w_router, b_router, w_gate_up, b_gate_up, w_down, b_down, final_g):
    depth = norm1_g.shape[0]
    assert depth == 1
    B, L, D = x_prompt.shape
    SB, SL, _ = x_sample.shape
    row = lambda v: v[0].reshape(1, -1)
    weights = (
        row(norm1_g), w_in[0].astype(BF16), w_pool_grp[0].astype(BF16), row(pool_scale),
        w_pool_out[0].astype(BF16),
        jnp.broadcast_to(dw_w[0][:, None, :], (CONV_WIDTH, SUBLANES, CONV_DIM)),
        row(dw_b), row(conv_ln_g), row(conv_ln_b),
        w_conv_out[0].astype(BF16), w_out[0].astype(BF16), row(norm2_g),
        w_router[0].astype(BF16), row(b_router),
    )
    zero_pool = jnp.zeros((B, POOL_HIST, POOL_WIDTH), F32)
    zero_conv = jnp.zeros((B, CONV_HIST, CONV_DIM), F32)
    zero_cnt = jnp.zeros((1, N_EXPERTS), F32)

    h_p, xn_p, meta_p, pool_p, conv_p, cnt_p = _front(
        x_prompt, zero_pool, zero_conv, zero_cnt, weights, 1, FRONT_TILE, 0)
    h_s, xn_s, meta_s, pool_s, conv_s, cnt_s = _front(
        x_sample, state_pool[0], state_conv[0], cnt_p, weights, SAMPLE_STREAMS, SL, L)

    n_tok = B * L + SB * SL
    n_tiles = (n_tok * TOP_K + N_EXPERTS * (EXPERT_TILE - 1)) // EXPERT_TILE
    counts = cnt_s[0].astype(I32)
    tiles_per_e = (counts + EXPERT_TILE - 1) // EXPERT_TILE
    tile_end = jnp.cumsum(tiles_per_e)
    group_start = (tile_end - tiles_per_e) * EXPERT_TILE
    tile_ids = jnp.arange(n_tiles, dtype=I32)
    tile_expert = jnp.minimum(
        jnp.sum((tile_ids[:, None] >= tile_end[None, :]).astype(I32), axis=1), N_EXPERTS - 1)
    tile_on = (tile_ids < tile_end[-1]).astype(I32)
    last_tile = (tile_end[-1:] - 1).astype(I32)
    expert_ids = jnp.arange(N_EXPERTS, dtype=I32)
    has_rows = tiles_per_e > 0
    later = (expert_ids[None, :] > expert_ids[:, None]) & has_rows[None, :]
    next_e = jnp.min(jnp.where(later, expert_ids[None, :], N_EXPERTS), axis=1)
    next_e = jnp.where(next_e < N_EXPERTS, next_e, -1).astype(I32)
    slot_e = ((jnp.cumsum(has_rows.astype(I32)) - 1) % 2).astype(I32)
    of_tile = tile_expert[:, None] == expert_ids[None, :]
    tile_next = jnp.sum(jnp.where(of_tile, next_e[None, :], 0), axis=1).astype(I32)
    tile_slot = jnp.sum(jnp.where(of_tile, slot_e[None, :], 0), axis=1).astype(I32)

    meta_p2 = meta_p.reshape(B * L, META_LANES)
    meta_s2 = meta_s.reshape(SB * SL, META_LANES)
    pos_p = _sorted_positions(meta_p2, group_start)
    pos_s = _sorted_positions(meta_s2, group_start)

    zero_rows = jnp.where(tiles_per_e > 0, (tile_end - 1) * EXPERT_TILE, -1).astype(I32)
    xs = _dispatch(jnp.concatenate([pos_p, pos_s]), zero_rows, tile_end[-1:].astype(I32),
                   xn_p.reshape(B * L, SUBLANES, LANES), xn_s.reshape(SB * SL, SUBLANES, LANES),
                   n_tiles)
    xs = xs.reshape(n_tiles * EXPERT_TILE * SUBLANES, LANES)
    ys = _experts(tile_expert, tile_on, last_tile, tile_next, tile_slot, xs,
                  w_gate_up[0], b_gate_up[0], w_down[0], b_down[0])
    y_p = _combine(pos_p, meta_p2, h_p.reshape(B * L, D), final_g, ys)
    y_s = _combine(pos_s, meta_s2, h_s.reshape(SB * SL, D), final_g, ys)
    return (y_p.reshape(B, L, D), y_s.reshape(SB, SL, D), pool_p, conv_p, pool_s, conv_s)
```

```python
import functools

import jax
import jax.numpy as jnp
from jax import lax
from jax.experimental import pallas as pl
from jax.experimental.pallas import tpu as pltpu

F32 = jnp.float32
BF16 = jnp.bfloat16
I32 = jnp.int32

D_MODEL = 1024
POOL_WINDOWS = (2, 4, 8, 16)
POOL_WIDTH = 512
POOL_GROUP_DIM = 128
POOL_HIST = 15
CONV_DIM = 1024
CONV_WIDTH = 31
CONV_HIST = 30
N_EXPERTS = 32
TOP_K = 4
D_FF = 1024
SWIGLU_LIMIT = 7.0
SWIGLU_ALPHA = 1.702
NORM_EPS = 1e-6

SUBLANES = 8
LANES = 128
POOL_BASE = 16
CONV_BASE = 32
META_LANES = 128
META_RANK = 4
META_WEIGHT = 8

FRONT_TILE = 256
SAMPLE_STREAMS = 8
TOKEN_TILE = 256
EXPERT_TILE = 256
CONV_CHUNK_ROWS = 16
VMEM_LIMIT = 56 * 1024 * 1024


def _sigmoid(v):
    return 1.0 / (1.0 + jnp.exp(-v))


def _dot(a, b):
    return jnp.dot(a, b, preferred_element_type=F32)


def _front_kernel(S, Lt, pos0,
                  x_ref, hp_ref, hc_ref, base_ref, g1_ref, win_ref, wg_ref, ps_ref, wpo_ref,
                  dww_ref, dwb_ref, lng_ref, lnb_ref, wco_ref, wo_ref, g2_ref, wr_ref, br_ref,
                  h_ref, xn_ref, meta_ref, np_ref, nc_ref, cnt_ref,
                  ep_ref, ec_ref, sb_ref, run_ref, band_ref):
    b = pl.program_id(0)
    l = pl.program_id(1)
    T = S * Lt
    P, C, D = POOL_WIDTH, CONV_DIM, D_MODEL
    tap0 = CONV_BASE - CONV_HIST
    first = l == 0

    @pl.when((b == 0) & first)
    def _():
        run_ref[...] = base_ref[...]
        col = lax.broadcasted_iota(I32, (Lt, POOL_BASE + Lt), 1)
        newest = lax.broadcasted_iota(I32, (Lt, POOL_BASE + Lt), 0) + POOL_BASE
        for g, w in enumerate(POOL_WINDOWS):
            band_ref[g] = ((col <= newest) & (col > newest - w)).astype(BF16)
        ep_ref[...] = jnp.zeros_like(ep_ref)
        ec_ref[...] = jnp.zeros_like(ec_ref)

    def conv_half():
        rc = min(CONV_CHUNK_ROWS, Lt)
        sc = CONV_CHUNK_ROWS // rc
        for s0 in range(0, S, sc):
            for r0 in range(0, Lt, rc):
                acc = None
                for j in range(CONV_WIDTH):
                    r = (tap0 + j) % SUBLANES
                    base = tap0 + j - r + r0
                    rows = ec_ref[r, s0:s0 + sc, base:base + rc, :]
                    term = rows.reshape(sc * rc // SUBLANES, SUBLANES, C) * dww_ref[j]
                    acc = term if acc is None else acc + term
                v = acc + dwb_ref[...]
                mu = jnp.mean(v, axis=-1, keepdims=True)
                vc = v - mu
                var = jnp.mean(vc * vc, axis=-1, keepdims=True)
                vn = vc * lax.rsqrt(var + NORM_EPS) * lng_ref[...] + lnb_ref[...]
                act = vn * _sigmoid(vn)
                row = s0 * Lt + r0
                sb_ref[row:row + sc * rc, :] = act.reshape(sc * rc, C).astype(BF16)

    def projection_half():
        ep_ref[:, POOL_BASE - POOL_HIST:POOL_BASE, :] = jnp.where(
            first, hp_ref[...], ep_ref[:, Lt + POOL_BASE - POOL_HIST:Lt + POOL_BASE, :])
        for r in range(SUBLANES):
            lo = max(0, tap0 - r)
            ec_ref[r, :, lo:CONV_BASE - r, :] = jnp.where(
                first, hc_ref[:, lo + r - tap0:CONV_HIST, :],
                ec_ref[r, :, Lt + lo:Lt + CONV_BASE - r, :])

        x = x_ref[...].reshape(T, D)
        n1 = x * lax.rsqrt(jnp.mean(x * x, axis=-1, keepdims=True) + NORM_EPS) * g1_ref[...]
        n1b = n1.astype(BF16)

        zp = _dot(n1b, win_ref[:, 0:P])
        ep_ref[:, POOL_BASE:POOL_BASE + Lt, :] = zp.reshape(S, Lt, P)
        np_ref[0] = ep_ref[:, Lt + POOL_BASE - POOL_HIST:Lt + POOL_BASE, :]
        pos = (pos0 + l * Lt + lax.broadcasted_iota(I32, (Lt, 1), 0)).astype(F32)
        ds = [[] for _ in POOL_WINDOWS]
        for s in range(S):
            ext = ep_ref[s]
            hi = ext.astype(BF16)
            lo_part = (ext - hi.astype(F32)).astype(BF16)
            for g, w in enumerate(POOL_WINDOWS):
                sl = slice(g * POOL_GROUP_DIM, (g + 1) * POOL_GROUP_DIM)
                parts = _dot(band_ref[g], jnp.concatenate([hi[:, sl], lo_part[:, sl]], axis=-1))
                total = parts[:, :POOL_GROUP_DIM] + parts[:, POOL_GROUP_DIM:]
                cnt = jnp.minimum(pos + 1.0, float(w))
                ds[g].append(total / cnt - ext[POOL_BASE:, sl])
        ys = []
        for g in range(len(POOL_WINDOWS)):
            d = ds[g][0] if S == 1 else jnp.concatenate(ds[g], axis=0)
            ys.append(_dot(d.astype(BF16), wg_ref[g]))
        yp = jnp.concatenate(ys, axis=-1) * ps_ref[...]
        a = _dot(yp.astype(BF16), wpo_ref[...])

        za = _dot(n1b, win_ref[:, P:P + C])
        zg = _dot(n1b, win_ref[:, P + C:P + 2 * C])
        u3 = (za * _sigmoid(zg)).reshape(S, Lt, C)
        for r in range(SUBLANES):
            ec_ref[r, :, CONV_BASE - r:CONV_BASE - r + Lt, :] = u3
        nc_ref[0] = ec_ref[tap0, :, Lt:Lt + CONV_HIST, :]

        ga = _sigmoid(_dot(n1b, win_ref[:, P + 2 * C:P + 2 * C + D]))
        gb = _sigmoid(_dot(n1b, win_ref[:, P + 2 * C + D:P + 2 * C + 2 * D]))
        return x, ga * a, gb

    x, gated_pool, gate_conv = projection_half()
    conv_half()
    _front_tail(S, Lt, x, gated_pool, gate_conv, sb_ref, wco_ref, wo_ref, g2_ref, wr_ref, br_ref,
                run_ref, h_ref, xn_ref, meta_ref, cnt_ref)


def _front_tail(S, Lt, x, gated_pool, gate_conv, sb_ref, wco_ref, wo_ref, g2_ref, wr_ref, br_ref,
                run_ref, h_ref, xn_ref, meta_ref, cnt_ref):
    T = S * Lt
    D = D_MODEL
    bb = _dot(sb_ref[...], wco_ref[...])
    m = gated_pool + gate_conv * bb
    h = x + _dot(m.astype(BF16), wo_ref[...])
    h_ref[...] = h.reshape(S, Lt, D)
    xn = h * lax.rsqrt(jnp.mean(h * h, axis=-1, keepdims=True) + NORM_EPS) * g2_ref[...]
    for s in range(SUBLANES):
        xn_ref[pl.ds(s, T, stride=SUBLANES), :] = xn[:, s * LANES:(s + 1) * LANES]

    logits = _dot(xn.astype(BF16), wr_ref[...]) + br_ref[...]
    lane_e = lax.broadcasted_iota(I32, (T, N_EXPERTS), 1).astype(F32)
    work = logits
    sels, tops, ids = [], [], []
    for _ in range(TOP_K):
        mx = jnp.max(work, axis=-1, keepdims=True)
        idx = jnp.min(jnp.where(work == mx, lane_e, float(N_EXPERTS)), axis=-1, keepdims=True)
        sel = lane_e == idx
        work = jnp.where(sel, -jnp.inf, work)
        sels.append(sel)
        tops.append(mx)
        ids.append(idx)
    exps = [jnp.exp(t - tops[0]) for t in tops]
    denom = exps[0] + exps[1] + exps[2] + exps[3]
    wts = [e / denom for e in exps]

    mask = jnp.zeros((T, N_EXPERTS), F32)
    for sel in sels:
        mask = mask + sel.astype(F32)
    tri = (lax.broadcasted_iota(I32, (T, T), 1) < lax.broadcasted_iota(I32, (T, T), 0)).astype(BF16)
    excl = _dot(tri, mask.astype(BF16)) + run_ref[...]
    run_new = run_ref[...] + jnp.sum(mask, axis=0, keepdims=True)
    run_ref[...] = run_new
    cnt_ref[...] = run_new

    lane_m = lax.broadcasted_iota(I32, (T, META_LANES), 1)
    meta = jnp.zeros((T, META_LANES), F32)
    for k in range(TOP_K):
        rank = jnp.sum(jnp.where(sels[k], excl, 0.0), axis=-1, keepdims=True)
        meta = jnp.where(lane_m == k, ids[k], meta)
        meta = jnp.where(lane_m == META_RANK + k, rank, meta)
        meta = jnp.where(lane_m == META_WEIGHT + k, wts[k], meta)
    meta_ref[...] = meta.reshape(S, Lt, META_LANES)


def _const_spec(shape):
    nd = len(shape)
    return pl.BlockSpec(shape, lambda b, l: (0,) * nd, pipeline_mode=pl.Buffered(1))


def _front(x, hist_pool, hist_conv, base_cnt, weights, S, Lt, pos0):
    B, L, D = x.shape
    P, C = POOL_WIDTH, CONV_DIM
    grid = (B // S, L // Lt)
    T = S * Lt
    tok = lambda b, l: (b, l, 0)
    per_b = lambda b, l: (b, 0, 0)
    in_specs = [
        pl.BlockSpec((S, Lt, D), tok),
        pl.BlockSpec((S, POOL_HIST, P), per_b),
        pl.BlockSpec((S, CONV_HIST, C), per_b),
        _const_spec((1, N_EXPERTS)),
    ] + [_const_spec(w.shape) for w in weights]
    out_shape = (
        jax.ShapeDtypeStruct((B, L, D), F32),
        jax.ShapeDtypeStruct((B * L * SUBLANES, LANES), F32),
        jax.ShapeDtypeStruct((B, L, META_LANES), F32),
        jax.ShapeDtypeStruct((1, B, POOL_HIST, P), F32),
        jax.ShapeDtypeStruct((1, B, CONV_HIST, C), F32),
        jax.ShapeDtypeStruct((1, N_EXPERTS), F32),
    )
    steps_l = L // Lt
    out_specs = (
        pl.BlockSpec((S, Lt, D), tok),
        pl.BlockSpec((T * SUBLANES, LANES), lambda b, l: (b * steps_l + l, 0)),
        pl.BlockSpec((S, Lt, META_LANES), tok),
        pl.BlockSpec((1, S, POOL_HIST, P), lambda b, l: (0, b, 0, 0)),
        pl.BlockSpec((1, S, CONV_HIST, C), lambda b, l: (0, b, 0, 0)),
        pl.BlockSpec((1, N_EXPERTS), lambda b, l: (0, 0)),
    )
    scratch = [
        pltpu.VMEM((S, POOL_BASE + Lt, P), F32),
        pltpu.VMEM((SUBLANES, S, CONV_BASE + Lt, C), F32),
        pltpu.VMEM((T, C), BF16),
        pltpu.VMEM((1, N_EXPERTS), F32),
        pltpu.VMEM((len(POOL_WINDOWS), Lt, POOL_BASE + Lt), BF16),
    ]
    return pl.pallas_call(
        functools.partial(_front_kernel, S, Lt, pos0),
        grid=grid, in_specs=in_specs, out_specs=out_specs, out_shape=out_shape,
        scratch_shapes=scratch,
        compiler_params=pltpu.CompilerParams(
            dimension_semantics=("arbitrary", "arbitrary"), vmem_limit_bytes=VMEM_LIMIT),
        name="front",
    )(x, hist_pool, hist_conv, base_cnt, *weights)


def _dispatch_kernel(T, steps_a, n_tiles, pos_ref, zrow_ref, used_ref, xa_ref, xb_ref, xs_ref,
                     sem, zbuf, zsem):
    i = pl.program_id(0)

    @pl.when(i == 0)
    def _():
        zbuf[...] = jnp.zeros_like(zbuf)

        def fill(row0):
            row0 = pl.multiple_of(row0, EXPERT_TILE)
            return pltpu.make_async_copy(zbuf, xs_ref.at[pl.ds(row0, EXPERT_TILE)], zsem)

        def tail_start(j, carry):
            fill(j * EXPERT_TILE).start()
            return carry

        def tail_wait(j, carry):
            fill(j * EXPERT_TILE).wait()
            return carry

        for e in range(N_EXPERTS):
            pl.when(zrow_ref[e] >= 0)(lambda e=e: fill(zrow_ref[e]).start())
        lax.fori_loop(used_ref[0], n_tiles, tail_start, 0)
        for e in range(N_EXPERTS):
            pl.when(zrow_ref[e] >= 0)(lambda e=e: fill(zrow_ref[e]).wait())
        lax.fori_loop(used_ref[0], n_tiles, tail_wait, 0)

    def scatter(src_ref):
        def start(t, carry):
            for k in range(TOP_K):
                pltpu.make_async_copy(
                    src_ref.at[t], xs_ref.at[pos_ref[TOP_K * t + k]], sem).start(priority=k % 2)
            return carry

        def wait(t, carry):
            for k in range(TOP_K):
                pltpu.make_async_copy(src_ref.at[0], xs_ref.at[0], sem).wait()
            return carry

        lax.fori_loop(0, T, start, 0, unroll=8)
        lax.fori_loop(0, T, wait, 0, unroll=8)

    pl.when(i < steps_a)(lambda: scatter(xa_ref))
    pl.when(i >= steps_a)(lambda: scatter(xb_ref))


def _dispatch(pos_flat, zero_rows, used_tiles, xn_a, xn_b, n_tiles):
    T = TOKEN_TILE
    steps_a, steps_b = xn_a.shape[0] // T, xn_b.shape[0] // T
    tile = (SUBLANES, LANES)
    return pl.pallas_call(
        functools.partial(_dispatch_kernel, T, steps_a, n_tiles),
        grid=(steps_a + steps_b,),
        in_specs=[
            pl.BlockSpec((T * TOP_K,), lambda i: (i,), memory_space=pltpu.SMEM),
            pl.BlockSpec(memory_space=pltpu.SMEM),
            pl.BlockSpec(memory_space=pltpu.SMEM),
            pl.BlockSpec((T,) + tile, lambda i: (jnp.minimum(i, steps_a - 1), 0, 0)),
            pl.BlockSpec((T,) + tile, lambda i: (jnp.maximum(i - steps_a, 0), 0, 0)),
        ],
        out_specs=pl.BlockSpec(memory_space=pl.ANY),
        out_shape=jax.ShapeDtypeStruct((n_tiles * EXPERT_TILE,) + tile, F32),
        scratch_shapes=[pltpu.SemaphoreType.DMA(()), pltpu.VMEM((EXPERT_TILE,) + tile, F32),
                        pltpu.SemaphoreType.DMA(())],
        compiler_params=pltpu.CompilerParams(dimension_semantics=("arbitrary",)),
        name="dispatch",
    )(pos_flat, zero_rows, used_tiles, xn_a, xn_b)


def _expert_kernel(te_ref, on_ref, last_ref, next_ref, slot_ref,
                   xs_ref, wgu_hbm, bgu_ref, wd_hbm, bd_ref, ys_ref,
                   wgu_f32, wd_f32, wgu_bf, wd_bf, sem_gu, sem_d):
    del last_ref
    i = pl.program_id(0)
    e = te_ref[i]
    slot = slot_ref[i]
    group_start = (i == 0) | (e != te_ref[jnp.maximum(i - 1, 0)])

    def weight_copies(expert, s):
        return (pltpu.make_async_copy(wgu_hbm.at[expert], wgu_f32.at[s], sem_gu.at[s]),
                pltpu.make_async_copy(wd_hbm.at[expert], wd_f32.at[s], sem_d.at[s]))

    @pl.when(i == 0)
    def _():
        for c in weight_copies(e, slot):
            c.start()

    @pl.when(group_start & (on_ref[i] == 1))
    def _():
        for c in weight_copies(e, slot):
            c.wait()

        @pl.when(next_ref[i] >= 0)
        def _():
            for c in weight_copies(next_ref[i], 1 - slot):
                c.start()

        wgu_bf[...] = wgu_f32[slot].astype(BF16)
        wd_bf[...] = wd_f32[slot].astype(BF16)

    @pl.when(on_ref[i] == 1)
    def _():
        xb = jnp.concatenate(
            [xs_ref[pl.ds(s, EXPERT_TILE, stride=SUBLANES), :] for s in range(SUBLANES)],
            axis=-1).astype(BF16)
        gu = _dot(xb, wgu_bf[...]) + bgu_ref[0]
        gate = jnp.minimum(gu[:, :D_FF], SWIGLU_LIMIT)
        up = jnp.clip(gu[:, D_FF:], -SWIGLU_LIMIT, SWIGLU_LIMIT)
        hm = (up + 1.0) * (gate * _sigmoid(SWIGLU_ALPHA * gate))
        y = _dot(hm.astype(BF16), wd_bf[...]) + bd_ref[0]
        for s in range(SUBLANES):
            ys_ref[pl.ds(s, EXPERT_TILE, stride=SUBLANES), :] = y[:, s * LANES:(s + 1) * LANES]

    @pl.when(on_ref[i] == 0)
    def _():
        ys_ref[...] = jnp.zeros_like(ys_ref)


def _experts(tile_expert, tile_on, last_tile, tile_next, tile_slot, xs, w_gate_up, b_gate_up,
             w_down, b_down):
    D = D_MODEL
    R = xs.shape[0] // SUBLANES
    TM = EXPERT_TILE
    row = lambda i, te, on, last, nxt, slot: (jnp.minimum(i, last[0]), 0)
    per_e = lambda i, te, on, last, nxt, slot: (te[i], 0, 0)
    grid_spec = pltpu.PrefetchScalarGridSpec(
        num_scalar_prefetch=5,
        grid=(R // TM,),
        in_specs=[
            pl.BlockSpec((TM * SUBLANES, LANES), row),
            pl.BlockSpec(memory_space=pl.ANY),
            pl.BlockSpec((1, 1, 2 * D_FF), per_e),
            pl.BlockSpec(memory_space=pl.ANY),
            pl.BlockSpec((1, 1, D), per_e),
        ],
        out_specs=pl.BlockSpec((TM * SUBLANES, LANES), lambda i, te, on, last, nxt, slot: (i, 0)),
        scratch_shapes=[
            pltpu.VMEM((2, D, 2 * D_FF), F32), pltpu.VMEM((2, D_FF, D), F32),
            pltpu.VMEM((D, 2 * D_FF), BF16), pltpu.VMEM((D_FF, D), BF16),
            pltpu.SemaphoreType.DMA((2,)), pltpu.SemaphoreType.DMA((2,)),
        ],
    )
    return pl.pallas_call(
        _expert_kernel,
        grid_spec=grid_spec,
        out_shape=jax.ShapeDtypeStruct((R * SUBLANES, LANES), F32),
        compiler_params=pltpu.CompilerParams(
            dimension_semantics=("arbitrary",), vmem_limit_bytes=VMEM_LIMIT),
        name="experts",
    )(tile_expert, tile_on, last_tile, tile_next, tile_slot, xs, w_gate_up,
      b_gate_up.reshape(N_EXPERTS, 1, 2 * D_FF), w_down, b_down.reshape(N_EXPERTS, 1, D))


def _combine_kernel(T, n_steps, pos_ref, pos_next_ref, meta_ref, h_ref, g_ref, ys_ref, out_ref,
                    gbuf, sem):
    i = pl.program_id(0)
    slot = i % 2

    def gather(rows_ref, s):
        def start(t, carry):
            row = pl.multiple_of(t * SUBLANES, SUBLANES)
            for k in range(TOP_K):
                pltpu.make_async_copy(
                    ys_ref.at[rows_ref[TOP_K * t + k]], gbuf.at[s, k, pl.ds(row, SUBLANES)],
                    sem.at[s]).start(priority=k % 2)
            return carry

        lax.fori_loop(0, T, start, 0, unroll=8)

    pl.when(i == 0)(lambda: gather(pos_ref, 0))
    pl.when(i + 1 < n_steps)(lambda: gather(pos_next_ref, 1 - slot))

    def wait(t, carry):
        for k in range(TOP_K):
            pltpu.make_async_copy(
                ys_ref.at[0], gbuf.at[slot, k, pl.ds(0, SUBLANES)], sem.at[slot]).wait()
        return carry

    lax.fori_loop(0, T, wait, 0, unroll=8)
    y = h_ref[...]
    for k in range(TOP_K):
        wk = meta_ref[:, META_WEIGHT + k:META_WEIGHT + k + 1]
        gk = jnp.concatenate(
            [gbuf[slot, k, pl.ds(s, T, stride=SUBLANES), :] for s in range(SUBLANES)], axis=-1)
        y = y + wk * gk
    out_ref[...] = y * lax.rsqrt(jnp.mean(y * y, axis=-1, keepdims=True) + NORM_EPS) * g_ref[...]


def _combine(pos_flat, meta, h, final_g, ys):
    N, D = h.shape
    T = TOKEN_TILE
    n_steps = N // T
    return pl.pallas_call(
        functools.partial(_combine_kernel, T, n_steps),
        grid=(n_steps,),
        in_specs=[
            pl.BlockSpec((T * TOP_K,), lambda i: (i,), memory_space=pltpu.SMEM),
            pl.BlockSpec((T * TOP_K,), lambda i: (jnp.minimum(i + 1, n_steps - 1),),
                         memory_space=pltpu.SMEM),
            pl.BlockSpec((T, META_LANES), lambda i: (i, 0)),
            pl.BlockSpec((T, D), lambda i: (i, 0)),
            pl.BlockSpec((1, D), lambda i: (0, 0)),
            pl.BlockSpec(memory_space=pl.ANY),
        ],
        out_specs=pl.BlockSpec((T, D), lambda i: (i, 0)),
        out_shape=jax.ShapeDtypeStruct((N, D), F32),
        scratch_shapes=[pltpu.VMEM((2, TOP_K, T * SUBLANES, LANES), F32),
                        pltpu.SemaphoreType.DMA((2,))],
        compiler_params=pltpu.CompilerParams(
            dimension_semantics=("arbitrary",), vmem_limit_bytes=VMEM_LIMIT),
        name="combine",
    )(pos_flat, pos_flat, meta, h, final_g.reshape(1, D), ys)


def _sorted_positions(meta, group_start):
    ids = meta[:, 0:TOP_K].astype(I32)
    rank = meta[:, META_RANK:META_RANK + TOP_K].astype(I32)
    experts = jnp.arange(N_EXPERTS, dtype=I32)
    start = jnp.sum(jnp.where(ids[:, :, None] == experts, group_start, 0), axis=-1)
    return (start + rank).reshape(-1)


def kernel(x_prompt, x_sample, state_pool, state_conv, norm1_g, w_in, w_pool_grp, pool_scale, w_pool_out, dw_w, dw_b, conv_ln_g, conv_ln_b, w_conv_out, w_out, norm2_g, w_router, b_router, w_gate_up, b_gate_up, w_down, b_down, final_g):
    depth = norm1_g.shape[0]
    assert depth == 1
    B, L, D = x_prompt.shape
    SB, SL, _ = x_sample.shape
    row = lambda v: v[0].reshape(1, -1)
    weights = (
        row(norm1_g), w_in[0].astype(BF16), w_pool_grp[0].astype(BF16), row(pool_scale),
        w_pool_out[0].astype(BF16),
        jnp.broadcast_to(dw_w[0][:, None, :], (CONV_WIDTH, SUBLANES, CONV_DIM)),
        row(dw_b), row(conv_ln_g), row(conv_ln_b),
        w_conv_out[0].astype(BF16), w_out[0].astype(BF16), row(norm2_g),
        w_router[0].astype(BF16), row(b_router),
    )
    zero_pool = jnp.zeros((B, POOL_HIST, POOL_WIDTH), F32)
    zero_conv = jnp.zeros((B, CONV_HIST, CONV_DIM), F32)
    zero_cnt = jnp.zeros((1, N_EXPERTS), F32)

    h_p, xn_p, meta_p, pool_p, conv_p, cnt_p = _front(
        x_prompt, zero_pool, zero_conv, zero_cnt, weights, 1, FRONT_TILE, 0)
    h_s, xn_s, meta_s, pool_s, conv_s, cnt_s = _front(
        x_sample, state_pool[0], state_conv[0], cnt_p, weights, SAMPLE_STREAMS, SL, L)

    n_tok = B * L + SB * SL
    n_tiles = (n_tok * TOP_K + N_EXPERTS * (EXPERT_TILE - 1)) // EXPERT_TILE
    counts = cnt_s[0].astype(I32)
    tiles_per_e = (counts + EXPERT_TILE - 1) // EXPERT_TILE
    tile_end = jnp.cumsum(tiles_per_e)
    group_start = (tile_end - tiles_per_e) * EXPERT_TILE
    tile_ids = jnp.arange(n_tiles, dtype=I32)
    tile_expert = jnp.minimum(
        jnp.sum((tile_ids[:, None] >= tile_end[None, :]).astype(I32), axis=1), N_EXPERTS - 1)
    tile_on = (tile_ids < tile_end[-1]).astype(I32)
    last_tile = (tile_end[-1:] - 1).astype(I32)
    expert_ids = jnp.arange(N_EXPERTS, dtype=I32)
    has_rows = tiles_per_e > 0
    later = (expert_ids[None, :] > expert_ids[:, None]) & has_rows[None, :]
    next_e = jnp.min(jnp.where(later, expert_ids[None, :], N_EXPERTS), axis=1)
    next_e = jnp.where(next_e < N_EXPERTS, next_e, -1).astype(I32)
    slot_e = ((jnp.cumsum(has_rows.astype(I32)) - 1) % 2).astype(I32)
    of_tile = tile_expert[:, None] == expert_ids[None, :]
    tile_next = jnp.sum(jnp.where(of_tile, next_e[None, :], 0), axis=1).astype(I32)
    tile_slot = jnp.sum(jnp.where(of_tile, slot_e[None, :], 0), axis=1).astype(I32)

    meta_p2 = meta_p.reshape(B * L, META_LANES)
    meta_s2 = meta_s.reshape(SB * SL, META_LANES)
    pos_p = _sorted_positions(meta_p2, group_start)
    pos_s = _sorted_positions(meta_s2, group_start)

    zero_rows = jnp.where(tiles_per_e > 0, (tile_end - 1) * EXPERT_TILE, -1).astype(I32)
    xs = _dispatch(jnp.concatenate([pos_p, pos_s]), zero_rows, tile_end[-1:].astype(I32),
                   xn_p.reshape(B * L, SUBLANES, LANES), xn_s.reshape(SB * SL, SUBLANES, LANES),
                   n_tiles)
    xs = xs.reshape(n_tiles * EXPERT_TILE * SUBLANES, LANES)
    ys = _experts(tile_expert, tile_on, last_tile, tile_next, tile_slot, xs,
                  w_gate_up[0], b_gate_up[0], w_down[0], b_down[0])
    ys = ys.reshape(n_tiles * EXPERT_TILE, SUBLANES, LANES)
    y_p = _combine(pos_p, meta_p2, h_p.reshape(B * L, D), final_g, ys)
    y_s = _combine(pos_s, meta_s2, h_s.reshape(SB * SL, D), final_g, ys)
    return (y_p.reshape(B, L, D), y_s.reshape(SB, SL, D), pool_p, conv_p, pool_s, conv_s)
```

```python
import functools

import jax
import jax.numpy as jnp
from jax import lax
from jax.experimental import pallas as pl
from jax.experimental.pallas import tpu as pltpu

F32 = jnp.float32
BF16 = jnp.bfloat16
I32 = jnp.int32

D_MODEL = 1024
POOL_WINDOWS = (2, 4, 8, 16)
POOL_WIDTH = 512
POOL_GROUP_DIM = 128
POOL_HIST = 15
CONV_DIM = 1024
CONV_WIDTH = 31
CONV_HIST = 30
N_EXPERTS = 32
TOP_K = 4
D_FF = 1024
SWIGLU_LIMIT = 7.0
SWIGLU_ALPHA = 1.702
NORM_EPS = 1e-6

SUBLANES = 8
LANES = 128
POOL_BASE = 16
CONV_BASE = 32
META_LANES = 128
META_RANK = 4
META_WEIGHT = 8

FRONT_TILE = 256
SAMPLE_STREAMS = 8
TOKEN_TILE = 256
EXPERT_TILE = 384
CONV_CHUNK_ROWS = 16
VMEM_LIMIT = 56 * 1024 * 1024


def _sigmoid(v):
    return 1.0 / (1.0 + jnp.exp(-v))


def _dot(a, b):
    return jnp.dot(a, b, preferred_element_type=F32)


def _front_kernel(S, Lt, pos0,
                  x_ref, hp_ref, hc_ref, base_ref, g1_ref, win_ref, wg_ref, ps_ref, wpo_ref,
                  dww_ref, dwb_ref, lng_ref, lnb_ref, wco_ref, wo_ref, g2_ref, wr_ref, br_ref,
                  h_ref, xn_ref, meta_ref, np_ref, nc_ref, cnt_ref,
                  ep_ref, ec_ref, sb_ref, run_ref, band_ref):
    b = pl.program_id(0)
    l = pl.program_id(1)
    T = S * Lt
    P, C, D = POOL_WIDTH, CONV_DIM, D_MODEL
    tap0 = CONV_BASE - CONV_HIST
    first = l == 0

    @pl.when((b == 0) & first)
    def _():
        run_ref[...] = base_ref[...]
        col = lax.broadcasted_iota(I32, (Lt, POOL_BASE + Lt), 1)
        newest = lax.broadcasted_iota(I32, (Lt, POOL_BASE + Lt), 0) + POOL_BASE
        for g, w in enumerate(POOL_WINDOWS):
            band_ref[g] = ((col <= newest) & (col > newest - w)).astype(BF16)
        ep_ref[...] = jnp.zeros_like(ep_ref)
        ec_ref[...] = jnp.zeros_like(ec_ref)

    def conv_half():
        rc = min(CONV_CHUNK_ROWS, Lt)
        sc = CONV_CHUNK_ROWS // rc
        for s0 in range(0, S, sc):
            for r0 in range(0, Lt, rc):
                acc = None
                for j in range(CONV_WIDTH):
                    r = (tap0 + j) % SUBLANES
                    base = tap0 + j - r + r0
                    rows = ec_ref[r, s0:s0 + sc, base:base + rc, :]
                    term = rows.reshape(sc * rc // SUBLANES, SUBLANES, C) * dww_ref[j]
                    acc = term if acc is None else acc + term
                v = acc + dwb_ref[...]
                mu = jnp.mean(v, axis=-1, keepdims=True)
                vc = v - mu
                var = jnp.mean(vc * vc, axis=-1, keepdims=True)
                vn = vc * lax.rsqrt(var + NORM_EPS) * lng_ref[...] + lnb_ref[...]
                act = vn * _sigmoid(vn)
                row = s0 * Lt + r0
                sb_ref[row:row + sc * rc, :] = act.reshape(sc * rc, C).astype(BF16)

    def projection_half():
        ep_ref[:, POOL_BASE - POOL_HIST:POOL_BASE, :] = jnp.where(
            first, hp_ref[...], ep_ref[:, Lt + POOL_BASE - POOL_HIST:Lt + POOL_BASE, :])
        for r in range(SUBLANES):
            lo = max(0, tap0 - r)
            ec_ref[r, :, lo:CONV_BASE - r, :] = jnp.where(
                first, hc_ref[:, lo + r - tap0:CONV_HIST, :],
                ec_ref[r, :, Lt + lo:Lt + CONV_BASE - r, :])

        x = x_ref[...].reshape(T, D)
        n1 = x * lax.rsqrt(jnp.mean(x * x, axis=-1, keepdims=True) + NORM_EPS) * g1_ref[...]
        n1b = n1.astype(BF16)

        zp = _dot(n1b, win_ref[:, 0:P])
        ep_ref[:, POOL_BASE:POOL_BASE + Lt, :] = zp.reshape(S, Lt, P)
        np_ref[0] = ep_ref[:, Lt + POOL_BASE - POOL_HIST:Lt + POOL_BASE, :]
        pos = (pos0 + l * Lt + lax.broadcasted_iota(I32, (Lt, 1), 0)).astype(F32)
        ds = [[] for _ in POOL_WINDOWS]
        for s in range(S):
            ext = ep_ref[s]
            hi = ext.astype(BF16)
            lo_part = (ext - hi.astype(F32)).astype(BF16)
            for g, w in enumerate(POOL_WINDOWS):
                sl = slice(g * POOL_GROUP_DIM, (g + 1) * POOL_GROUP_DIM)
                parts = _dot(band_ref[g], jnp.concatenate([hi[:, sl], lo_part[:, sl]], axis=-1))
                total = parts[:, :POOL_GROUP_DIM] + parts[:, POOL_GROUP_DIM:]
                cnt = jnp.minimum(pos + 1.0, float(w))
                ds[g].append(total / cnt - ext[POOL_BASE:, sl])
        ys = []
        for g in range(len(POOL_WINDOWS)):
            d = ds[g][0] if S == 1 else jnp.concatenate(ds[g], axis=0)
            ys.append(_dot(d.astype(BF16), wg_ref[g]))
        yp = jnp.concatenate(ys, axis=-1) * ps_ref[...]
        a = _dot(yp.astype(BF16), wpo_ref[...])

        za = _dot(n1b, win_ref[:, P:P + C])
        zg = _dot(n1b, win_ref[:, P + C:P + 2 * C])
        u3 = (za * _sigmoid(zg)).reshape(S, Lt, C)
        for r in range(SUBLANES):
            ec_ref[r, :, CONV_BASE - r:CONV_BASE - r + Lt, :] = u3
        nc_ref[0] = ec_ref[tap0, :, Lt:Lt + CONV_HIST, :]

        ga = _sigmoid(_dot(n1b, win_ref[:, P + 2 * C:P + 2 * C + D]))
        gb = _sigmoid(_dot(n1b, win_ref[:, P + 2 * C + D:P + 2 * C + 2 * D]))
        return x, ga * a, gb

    x, gated_pool, gate_conv = projection_half()
    conv_half()
    _front_tail(S, Lt, x, gated_pool, gate_conv, sb_ref, wco_ref, wo_ref, g2_ref, wr_ref, br_ref,
                run_ref, h_ref, xn_ref, meta_ref, cnt_ref)


def _front_tail(S, Lt, x, gated_pool, gate_conv, sb_ref, wco_ref, wo_ref, g2_ref, wr_ref, br_ref,
                run_ref, h_ref, xn_ref, meta_ref, cnt_ref):
    T = S * Lt
    D = D_MODEL
    bb = _dot(sb_ref[...], wco_ref[...])
    m = gated_pool + gate_conv * bb
    h = x + _dot(m.astype(BF16), wo_ref[...])
    h_ref[...] = h.reshape(S, Lt, D)
    xn = h * lax.rsqrt(jnp.mean(h * h, axis=-1, keepdims=True) + NORM_EPS) * g2_ref[...]
    for s in range(SUBLANES):
        xn_ref[pl.ds(s, T, stride=SUBLANES), :] = xn[:, s * LANES:(s + 1) * LANES]

    logits = _dot(xn.astype(BF16), wr_ref[...]) + br_ref[...]
    lane_e = lax.broadcasted_iota(I32, (T, N_EXPERTS), 1).astype(F32)
    work = logits
    sels, tops, ids = [], [], []
    for _ in range(TOP_K):
        mx = jnp.max(work, axis=-1, keepdims=True)
        idx = jnp.min(jnp.where(work == mx, lane_e, float(N_EXPERTS)), axis=-1, keepdims=True)
        sel = lane_e == idx
        work = jnp.where(sel, -jnp.inf, work)
        sels.append(sel)
        tops.append(mx)
        ids.append(idx)
    exps = [jnp.exp(t - tops[0]) for t in tops]
    denom = exps[0] + exps[1] + exps[2] + exps[3]
    wts = [e / denom for e in exps]

    mask = jnp.zeros((T, N_EXPERTS), F32)
    for sel in sels:
        mask = mask + sel.astype(F32)
    tri = (lax.broadcasted_iota(I32, (T, T), 1) < lax.broadcasted_iota(I32, (T, T), 0)).astype(BF16)
    excl = _dot(tri, mask.astype(BF16)) + run_ref[...]
    run_new = run_ref[...] + jnp.sum(mask, axis=0, keepdims=True)
    run_ref[...] = run_new
    cnt_ref[...] = run_new

    lane_m = lax.broadcasted_iota(I32, (T, META_LANES), 1)
    meta = jnp.zeros((T, META_LANES), F32)
    for k in range(TOP_K):
        rank = jnp.sum(jnp.where(sels[k], excl, 0.0), axis=-1, keepdims=True)
        meta = jnp.where(lane_m == k, ids[k], meta)
        meta = jnp.where(lane_m == META_RANK + k, rank, meta)
        meta = jnp.where(lane_m == META_WEIGHT + k, wts[k], meta)
    meta_ref[...] = meta.reshape(S, Lt, META_LANES)


def _const_spec(shape):
    nd = len(shape)
    return pl.BlockSpec(shape, lambda b, l: (0,) * nd, pipeline_mode=pl.Buffered(1))


def _front(x, hist_pool, hist_conv, base_cnt, weights, S, Lt, pos0):
    B, L, D = x.shape
    P, C = POOL_WIDTH, CONV_DIM
    grid = (B // S, L // Lt)
    T = S * Lt
    tok = lambda b, l: (b, l, 0)
    per_b = lambda b, l: (b, 0, 0)
    in_specs = [
        pl.BlockSpec((S, Lt, D), tok),
        pl.BlockSpec((S, POOL_HIST, P), per_b),
        pl.BlockSpec((S, CONV_HIST, C), per_b),
        _const_spec((1, N_EXPERTS)),
    ] + [_const_spec(w.shape) for w in weights]
    out_shape = (
        jax.ShapeDtypeStruct((B, L, D), F32),
        jax.ShapeDtypeStruct((B * L * SUBLANES, LANES), F32),
        jax.ShapeDtypeStruct((B, L, META_LANES), F32),
        jax.ShapeDtypeStruct((1, B, POOL_HIST, P), F32),
        jax.ShapeDtypeStruct((1, B, CONV_HIST, C), F32),
        jax.ShapeDtypeStruct((1, N_EXPERTS), F32),
    )
    steps_l = L // Lt
    out_specs = (
        pl.BlockSpec((S, Lt, D), tok),
        pl.BlockSpec((T * SUBLANES, LANES), lambda b, l: (b * steps_l + l, 0)),
        pl.BlockSpec((S, Lt, META_LANES), tok),
        pl.BlockSpec((1, S, POOL_HIST, P), lambda b, l: (0, b, 0, 0)),
        pl.BlockSpec((1, S, CONV_HIST, C), lambda b, l: (0, b, 0, 0)),
        pl.BlockSpec((1, N_EXPERTS), lambda b, l: (0, 0)),
    )
    scratch = [
        pltpu.VMEM((S, POOL_BASE + Lt, P), F32),
        pltpu.VMEM((SUBLANES, S, CONV_BASE + Lt, C), F32),
        pltpu.VMEM((T, C), BF16),
        pltpu.VMEM((1, N_EXPERTS), F32),
        pltpu.VMEM((len(POOL_WINDOWS), Lt, POOL_BASE + Lt), BF16),
    ]
    return pl.pallas_call(
        functools.partial(_front_kernel, S, Lt, pos0),
        grid=grid, in_specs=in_specs, out_specs=out_specs, out_shape=out_shape,
        scratch_shapes=scratch,
        compiler_params=pltpu.CompilerParams(
            dimension_semantics=("arbitrary", "arbitrary"), vmem_limit_bytes=VMEM_LIMIT),
        name="front",
    )(x, hist_pool, hist_conv, base_cnt, *weights)


def _dispatch_kernel(T, steps_a, n_tiles, pos_ref, zrow_ref, used_ref, xa_ref, xb_ref, xs_ref,
                     sem, zbuf, zsem):
    i = pl.program_id(0)

    @pl.when(i == 0)
    def _():
        zbuf[...] = jnp.zeros_like(zbuf)

        def fill(row0):
            row0 = pl.multiple_of(row0, EXPERT_TILE)
            return pltpu.make_async_copy(zbuf, xs_ref.at[pl.ds(row0, EXPERT_TILE)], zsem)

        def tail_start(j, carry):
            fill(j * EXPERT_TILE).start()
            return carry

        def tail_wait(j, carry):
            fill(j * EXPERT_TILE).wait()
            return carry

        for e in range(N_EXPERTS):
            pl.when(zrow_ref[e] >= 0)(lambda e=e: fill(zrow_ref[e]).start())
        lax.fori_loop(used_ref[0], n_tiles, tail_start, 0)
        for e in range(N_EXPERTS):
            pl.when(zrow_ref[e] >= 0)(lambda e=e: fill(zrow_ref[e]).wait())
        lax.fori_loop(used_ref[0], n_tiles, tail_wait, 0)

    def scatter(src_ref):
        def start(t, carry):
            for k in range(TOP_K):
                pltpu.make_async_copy(
                    src_ref.at[t], xs_ref.at[pos_ref[TOP_K * t + k]], sem).start(priority=k % 2)
            return carry

        def wait(t, carry):
            for k in range(TOP_K):
                pltpu.make_async_copy(src_ref.at[0], xs_ref.at[0], sem).wait()
            return carry

        lax.fori_loop(0, T, start, 0, unroll=8)
        lax.fori_loop(0, T, wait, 0, unroll=8)

    pl.when(i < steps_a)(lambda: scatter(xa_ref))
    pl.when(i >= steps_a)(lambda: scatter(xb_ref))


def _dispatch(pos_flat, zero_rows, used_tiles, xn_a, xn_b, n_tiles):
    T = TOKEN_TILE
    steps_a, steps_b = xn_a.shape[0] // T, xn_b.shape[0] // T
    tile = (SUBLANES, LANES)
    return pl.pallas_call(
        functools.partial(_dispatch_kernel, T, steps_a, n_tiles),
        grid=(steps_a + steps_b,),
        in_specs=[
            pl.BlockSpec((T * TOP_K,), lambda i: (i,), memory_space=pltpu.SMEM),
            pl.BlockSpec(memory_space=pltpu.SMEM),
            pl.BlockSpec(memory_space=pltpu.SMEM),
            pl.BlockSpec((T,) + tile, lambda i: (jnp.minimum(i, steps_a - 1), 0, 0)),
            pl.BlockSpec((T,) + tile, lambda i: (jnp.maximum(i - steps_a, 0), 0, 0)),
        ],
        out_specs=pl.BlockSpec(memory_space=pl.ANY),
        out_shape=jax.ShapeDtypeStruct((n_tiles * EXPERT_TILE,) + tile, F32),
        scratch_shapes=[pltpu.SemaphoreType.DMA(()), pltpu.VMEM((EXPERT_TILE,) + tile, F32),
                        pltpu.SemaphoreType.DMA(())],
        compiler_params=pltpu.CompilerParams(dimension_semantics=("arbitrary",)),
        name="dispatch",
    )(pos_flat, zero_rows, used_tiles, xn_a, xn_b)


def _expert_kernel(te_ref, on_ref, last_ref, next_ref, slot_ref,
                   xs_ref, wgu_hbm, bgu_ref, wd_hbm, bd_ref, ys_ref,
                   wgu_f32, wd_f32, wgu_bf, wd_bf, sem_gu, sem_d):
    del last_ref
    i = pl.program_id(0)
    e = te_ref[i]
    slot = slot_ref[i]
    group_start = (i == 0) | (e != te_ref[jnp.maximum(i - 1, 0)])

    def weight_copies(expert, s):
        return (pltpu.make_async_copy(wgu_hbm.at[expert], wgu_f32.at[s], sem_gu.at[s]),
                pltpu.make_async_copy(wd_hbm.at[expert], wd_f32.at[s], sem_d.at[s]))

    @pl.when(i == 0)
    def _():
        for c in weight_copies(e, slot):
            c.start()

    @pl.when(group_start & (on_ref[i] == 1))
    def _():
        for c in weight_copies(e, slot):
            c.wait()

        @pl.when(next_ref[i] >= 0)
        def _():
            for c in weight_copies(next_ref[i], 1 - slot):
                c.start()

        wgu_bf[...] = wgu_f32[slot].astype(BF16)
        wd_bf[...] = wd_f32[slot].astype(BF16)

    @pl.when(on_ref[i] == 1)
    def _():
        xb = jnp.concatenate(
            [xs_ref[pl.ds(s, EXPERT_TILE, stride=SUBLANES), :] for s in range(SUBLANES)],
            axis=-1).astype(BF16)
        gu = _dot(xb, wgu_bf[...]) + bgu_ref[0]
        gate = jnp.minimum(gu[:, :D_FF], SWIGLU_LIMIT)
        up = jnp.clip(gu[:, D_FF:], -SWIGLU_LIMIT, SWIGLU_LIMIT)
        hm = (up + 1.0) * (gate * _sigmoid(SWIGLU_ALPHA * gate))
        y = _dot(hm.astype(BF16), wd_bf[...]) + bd_ref[0]
        for s in range(SUBLANES):
            ys_ref[pl.ds(s, EXPERT_TILE, stride=SUBLANES), :] = y[:, s * LANES:(s + 1) * LANES]

    @pl.when(on_ref[i] == 0)
    def _():
        ys_ref[...] = jnp.zeros_like(ys_ref)


def _experts(tile_expert, tile_on, last_tile, tile_next, tile_slot, xs, w_gate_up, b_gate_up,
             w_down, b_down):
    D = D_MODEL
    R = xs.shape[0] // SUBLANES
    TM = EXPERT_TILE
    row = lambda i, te, on, last, nxt, slot: (jnp.minimum(i, last[0]), 0)
    per_e = lambda i, te, on, last, nxt, slot: (te[i], 0, 0)
    grid_spec = pltpu.PrefetchScalarGridSpec(
        num_scalar_prefetch=5,
        grid=(R // TM,),
        in_specs=[
            pl.BlockSpec((TM * SUBLANES, LANES), row),
            pl.BlockSpec(memory_space=pl.ANY),
            pl.BlockSpec((1, 1, 2 * D_FF), per_e),
            pl.BlockSpec(memory_space=pl.ANY),
            pl.BlockSpec((1, 1, D), per_e),
        ],
        out_specs=pl.BlockSpec((TM * SUBLANES, LANES), lambda i, te, on, last, nxt, slot: (i, 0)),
        scratch_shapes=[
            pltpu.VMEM((2, D, 2 * D_FF), F32), pltpu.VMEM((2, D_FF, D), F32),
            pltpu.VMEM((D, 2 * D_FF), BF16), pltpu.VMEM((D_FF, D), BF16),
            pltpu.SemaphoreType.DMA((2,)), pltpu.SemaphoreType.DMA((2,)),
        ],
    )
    return pl.pallas_call(
        _expert_kernel,
        grid_spec=grid_spec,
        out_shape=jax.ShapeDtypeStruct((R * SUBLANES, LANES), F32),
        compiler_params=pltpu.CompilerParams(
            dimension_semantics=("arbitrary",), vmem_limit_bytes=VMEM_LIMIT),
        name="experts",
    )(tile_expert, tile_on, last_tile, tile_next, tile_slot, xs, w_gate_up,
      b_gate_up.reshape(N_EXPERTS, 1, 2 * D_FF), w_down, b_down.reshape(N_EXPERTS, 1, D))


def _combine_kernel(T, n_steps, pos_ref, pos_next_ref, meta_ref, h_ref, g_ref, ys_ref, out_ref,
                    gbuf, sem):
    i = pl.program_id(0)
    slot = i % 2

    def gather(rows_ref, s):
        def start(t, carry):
            row = pl.multiple_of(t * SUBLANES, SUBLANES)
            for k in range(TOP_K):
                pltpu.make_async_copy(
                    ys_ref.at[rows_ref[TOP_K * t + k]], gbuf.at[s, k, pl.ds(row, SUBLANES)],
                    sem.at[s]).start(priority=k % 2)
            return carry

        lax.fori_loop(0, T, start, 0, unroll=8)

    pl.when(i == 0)(lambda: gather(pos_ref, 0))
    pl.when(i + 1 < n_steps)(lambda: gather(pos_next_ref, 1 - slot))

    def wait(t, carry):
        for k in range(TOP_K):
            pltpu.make_async_copy(
                ys_ref.at[0], gbuf.at[slot, k, pl.ds(0, SUBLANES)], sem.at[slot]).wait()
        return carry

    lax.fori_loop(0, T, wait, 0, unroll=8)
    y = h_ref[...]
    for k in range(TOP_K):
        wk = meta_ref[:, META_WEIGHT + k:META_WEIGHT + k + 1]
        gk = jnp.concatenate(
            [gbuf[slot, k, pl.ds(s, T, stride=SUBLANES), :] for s in range(SUBLANES)], axis=-1)
        y = y + wk * gk
    out_ref[...] = y * lax.rsqrt(jnp.mean(y * y, axis=-1, keepdims=True) + NORM_EPS) * g_ref[...]


def _combine(pos_flat, meta, h, final_g, ys):
    N, D = h.shape
    T = TOKEN_TILE
    n_steps = N // T
    return pl.pallas_call(
        functools.partial(_combine_kernel, T, n_steps),
        grid=(n_steps,),
        in_specs=[
            pl.BlockSpec((T * TOP_K,), lambda i: (i,), memory_space=pltpu.SMEM),
            pl.BlockSpec((T * TOP_K,), lambda i: (jnp.minimum(i + 1, n_steps - 1),),
                         memory_space=pltpu.SMEM),
            pl.BlockSpec((T, META_LANES), lambda i: (i, 0)),
            pl.BlockSpec((T, D), lambda i: (i, 0)),
            pl.BlockSpec((1, D), lambda i: (0, 0)),
            pl.BlockSpec(memory_space=pl.ANY),
        ],
        out_specs=pl.BlockSpec((T, D), lambda i: (i, 0)),
        out_shape=jax.ShapeDtypeStruct((N, D), F32),
        scratch_shapes=[pltpu.VMEM((2, TOP_K, T * SUBLANES, LANES), F32),
                        pltpu.SemaphoreType.DMA((2,))],
        compiler_params=pltpu.CompilerParams(
            dimension_semantics=("arbitrary",), vmem_limit_bytes=VMEM_LIMIT),
        name="combine",
    )(pos_flat, pos_flat, meta, h, final_g.reshape(1, D), ys)


def _sorted_positions(meta, group_start):
    ids = meta[:, 0:TOP_K].astype(I32)
    rank = meta[:, META_RANK:META_RANK + TOP_K].astype(I32)
    experts = jnp.arange(N_EXPERTS, dtype=I32)
    start = jnp.sum(jnp.where(ids[:, :, None] == experts, group_start, 0), axis=-1)
    return (start + rank).reshape(-1)


def kernel(x_prompt, x_sample, state_pool, state_conv, norm1_g, w_in, w_pool_grp, pool_scale, w_pool_out, dw_w, dw_b, conv_ln_g, conv_ln_b, w_conv_out, w_out, norm2_g, w_router, b_router, w_gate_up, b_gate_up, w_down, b_down, final_g):
    depth = norm1_g.shape[0]
    assert depth == 1
    B, L, D = x_prompt.shape
    SB, SL, _ = x_sample.shape
    row = lambda v: v[0].reshape(1, -1)
    weights = (
        row(norm1_g), w_in[0].astype(BF16), w_pool_grp[0].astype(BF16), row(pool_scale),
        w_pool_out[0].astype(BF16),
        jnp.broadcast_to(dw_w[0][:, None, :], (CONV_WIDTH, SUBLANES, CONV_DIM)),
        row(dw_b), row(conv_ln_g), row(conv_ln_b),
        w_conv_out[0].astype(BF16), w_out[0].astype(BF16), row(norm2_g),
        w_router[0].astype(BF16), row(b_router),
    )
    zero_pool = jnp.zeros((B, POOL_HIST, POOL_WIDTH), F32)
    zero_conv = jnp.zeros((B, CONV_HIST, CONV_DIM), F32)
    zero_cnt = jnp.zeros((1, N_EXPERTS), F32)

    h_p, xn_p, meta_p, pool_p, conv_p, cnt_p = _front(
        x_prompt, zero_pool, zero_conv, zero_cnt, weights, 1, FRONT_TILE, 0)
    h_s, xn_s, meta_s, pool_s, conv_s, cnt_s = _front(
        x_sample, state_pool[0], state_conv[0], cnt_p, weights, SAMPLE_STREAMS, SL, L)

    n_tok = B * L + SB * SL
    n_tiles = (n_tok * TOP_K + N_EXPERTS * (EXPERT_TILE - 1)) // EXPERT_TILE
    counts = cnt_s[0].astype(I32)
    tiles_per_e = (counts + EXPERT_TILE - 1) // EXPERT_TILE
    tile_end = jnp.cumsum(tiles_per_e)
    group_start = (tile_end - tiles_per_e) * EXPERT_TILE
    tile_ids = jnp.arange(n_tiles, dtype=I32)
    tile_expert = jnp.minimum(
        jnp.sum((tile_ids[:, None] >= tile_end[None, :]).astype(I32), axis=1), N_EXPERTS - 1)
    tile_on = (tile_ids < tile_end[-1]).astype(I32)
    last_tile = (tile_end[-1:] - 1).astype(I32)
    expert_ids = jnp.arange(N_EXPERTS, dtype=I32)
    has_rows = tiles_per_e > 0
    later = (expert_ids[None, :] > expert_ids[:, None]) & has_rows[None, :]
    next_e = jnp.min(jnp.where(later, expert_ids[None, :], N_EXPERTS), axis=1)
    next_e = jnp.where(next_e < N_EXPERTS, next_e, -1).astype(I32)
    slot_e = ((jnp.cumsum(has_rows.astype(I32)) - 1) % 2).astype(I32)
    of_tile = tile_expert[:, None] == expert_ids[None, :]
    tile_next = jnp.sum(jnp.where(of_tile, next_e[None, :], 0), axis=1).astype(I32)
    tile_slot = jnp.sum(jnp.where(of_tile, slot_e[None, :], 0), axis=1).astype(I32)

    meta_p2 = meta_p.reshape(B * L, META_LANES)
    meta_s2 = meta_s.reshape(SB * SL, META_LANES)
    pos_p = _sorted_positions(meta_p2, group_start)
    pos_s = _sorted_positions(meta_s2, group_start)

    zero_rows = jnp.where(tiles_per_e > 0, (tile_end - 1) * EXPERT_TILE, -1).astype(I32)
    xs = _dispatch(jnp.concatenate([pos_p, pos_s]), zero_rows, tile_end[-1:].astype(I32),
                   xn_p.reshape(B * L, SUBLANES, LANES), xn_s.reshape(SB * SL, SUBLANES, LANES),
                   n_tiles)
    xs = xs.reshape(n_tiles * EXPERT_TILE * SUBLANES, LANES)
    ys = _experts(tile_expert, tile_on, last_tile, tile_next, tile_slot, xs,
                  w_gate_up[0], b_gate_up[0], w_down[0], b_down[0])
    ys = ys.reshape(n_tiles * EXPERT_TILE, SUBLANES, LANES)
    y_p = _combine(pos_p, meta_p2, h_p.reshape(B * L, D), final_g, ys)
    y_s = _combine(pos_s, meta_s2, h_s.reshape(SB * SL, D), final_g, ys)
    return (y_p.reshape(B, L, D), y_s.reshape(SB, SL, D), pool_p, conv_p, pool_s, conv_s)
```

```python
import functools

import jax
import jax.numpy as jnp
from jax import lax
from jax.experimental import pallas as pl
from jax.experimental.pallas import tpu as pltpu

F32 = jnp.float32
BF16 = jnp.bfloat16
I32 = jnp.int32

D_MODEL = 1024
POOL_WINDOWS = (2, 4, 8, 16)
POOL_WIDTH = 512
POOL_GROUP_DIM = 128
POOL_HIST = 15
CONV_DIM = 1024
CONV_WIDTH = 31
CONV_HIST = 30
N_EXPERTS = 32
TOP_K = 4
D_FF = 1024
SWIGLU_LIMIT = 7.0
SWIGLU_ALPHA = 1.702
NORM_EPS = 1e-6

SUBLANES = 8
LANES = 128
POOL_BASE = 16
CONV_BASE = 32
META_LANES = 128
META_RANK = 4
META_WEIGHT = 8

FRONT_TILE = 256
SAMPLE_STREAMS = 8
TOKEN_TILE = 256
EXPERT_TILE = 768
CONV_CHUNK_ROWS = 16
VMEM_LIMIT = 56 * 1024 * 1024


def _sigmoid(v):
    return 1.0 / (1.0 + jnp.exp(-v))


def _dot(a, b):
    return jnp.dot(a, b, preferred_element_type=F32)


def _front_kernel(S, Lt, pos0,
                  x_ref, hp_ref, hc_ref, base_ref, g1_ref, win_ref, wg_ref, ps_ref, wpo_ref,
                  dww_ref, dwb_ref, lng_ref, lnb_ref, wco_ref, wo_ref, g2_ref, wr_ref, br_ref,
                  h_ref, xn_ref, meta_ref, np_ref, nc_ref, cnt_ref,
                  ep_ref, ec_ref, sb_ref, run_ref, band_ref):
    b = pl.program_id(0)
    l = pl.program_id(1)
    T = S * Lt
    P, C, D = POOL_WIDTH, CONV_DIM, D_MODEL
    tap0 = CONV_BASE - CONV_HIST
    first = l == 0

    @pl.when((b == 0) & first)
    def _():
        run_ref[...] = base_ref[...]
        col = lax.broadcasted_iota(I32, (Lt, POOL_BASE + Lt), 1)
        newest = lax.broadcasted_iota(I32, (Lt, POOL_BASE + Lt), 0) + POOL_BASE
        for g, w in enumerate(POOL_WINDOWS):
            band_ref[g] = ((col <= newest) & (col > newest - w)).astype(BF16)
        ep_ref[...] = jnp.zeros_like(ep_ref)
        ec_ref[...] = jnp.zeros_like(ec_ref)

    def conv_half():
        rc = min(CONV_CHUNK_ROWS, Lt)
        sc = CONV_CHUNK_ROWS // rc
        for s0 in range(0, S, sc):
            for r0 in range(0, Lt, rc):
                acc = None
                for j in range(CONV_WIDTH):
                    r = (tap0 + j) % SUBLANES
                    base = tap0 + j - r + r0
                    rows = ec_ref[r, s0:s0 + sc, base:base + rc, :]
                    term = rows.reshape(sc * rc // SUBLANES, SUBLANES, C) * dww_ref[j]
                    acc = term if acc is None else acc + term
                v = acc + dwb_ref[...]
                mu = jnp.mean(v, axis=-1, keepdims=True)
                vc = v - mu
                var = jnp.mean(vc * vc, axis=-1, keepdims=True)
                vn = vc * lax.rsqrt(var + NORM_EPS) * lng_ref[...] + lnb_ref[...]
                act = vn * _sigmoid(vn)
                row = s0 * Lt + r0
                sb_ref[row:row + sc * rc, :] = act.reshape(sc * rc, C).astype(BF16)

    def projection_half():
        ep_ref[:, POOL_BASE - POOL_HIST:POOL_BASE, :] = jnp.where(
            first, hp_ref[...], ep_ref[:, Lt + POOL_BASE - POOL_HIST:Lt + POOL_BASE, :])
        for r in range(SUBLANES):
            lo = max(0, tap0 - r)
            ec_ref[r, :, lo:CONV_BASE - r, :] = jnp.where(
                first, hc_ref[:, lo + r - tap0:CONV_HIST, :],
                ec_ref[r, :, Lt + lo:Lt + CONV_BASE - r, :])

        x = x_ref[...].reshape(T, D)
        n1 = x * lax.rsqrt(jnp.mean(x * x, axis=-1, keepdims=True) + NORM_EPS) * g1_ref[...]
        n1b = n1.astype(BF16)

        zp = _dot(n1b, win_ref[:, 0:P])
        ep_ref[:, POOL_BASE:POOL_BASE + Lt, :] = zp.reshape(S, Lt, P)
        np_ref[0] = ep_ref[:, Lt + POOL_BASE - POOL_HIST:Lt + POOL_BASE, :]
        pos = (pos0 + l * Lt + lax.broadcasted_iota(I32, (Lt, 1), 0)).astype(F32)
        ds = [[] for _ in POOL_WINDOWS]
        for s in range(S):
            ext = ep_ref[s]
            hi = ext.astype(BF16)
            lo_part = (ext - hi.astype(F32)).astype(BF16)
            for g, w in enumerate(POOL_WINDOWS):
                sl = slice(g * POOL_GROUP_DIM, (g + 1) * POOL_GROUP_DIM)
                parts = _dot(band_ref[g], jnp.concatenate([hi[:, sl], lo_part[:, sl]], axis=-1))
                total = parts[:, :POOL_GROUP_DIM] + parts[:, POOL_GROUP_DIM:]
                cnt = jnp.minimum(pos + 1.0, float(w))
                ds[g].append(total / cnt - ext[POOL_BASE:, sl])
        ys = []
        for g in range(len(POOL_WINDOWS)):
            d = ds[g][0] if S == 1 else jnp.concatenate(ds[g], axis=0)
            ys.append(_dot(d.astype(BF16), wg_ref[g]))
        yp = jnp.concatenate(ys, axis=-1) * ps_ref[...]
        a = _dot(yp.astype(BF16), wpo_ref[...])

        za = _dot(n1b, win_ref[:, P:P + C])
        zg = _dot(n1b, win_ref[:, P + C:P + 2 * C])
        u3 = (za * _sigmoid(zg)).reshape(S, Lt, C)
        for r in range(SUBLANES):
            ec_ref[r, :, CONV_BASE - r:CONV_BASE - r + Lt, :] = u3
        nc_ref[0] = ec_ref[tap0, :, Lt:Lt + CONV_HIST, :]

        ga = _sigmoid(_dot(n1b, win_ref[:, P + 2 * C:P + 2 * C + D]))
        gb = _sigmoid(_dot(n1b, win_ref[:, P + 2 * C + D:P + 2 * C + 2 * D]))
        return x, ga * a, gb

    x, gated_pool, gate_conv = projection_half()
    conv_half()
    _front_tail(S, Lt, x, gated_pool, gate_conv, sb_ref, wco_ref, wo_ref, g2_ref, wr_ref, br_ref,
                run_ref, h_ref, xn_ref, meta_ref, cnt_ref)


def _front_tail(S, Lt, x, gated_pool, gate_conv, sb_ref, wco_ref, wo_ref, g2_ref, wr_ref, br_ref,
                run_ref, h_ref, xn_ref, meta_ref, cnt_ref):
    T = S * Lt
    D = D_MODEL
    bb = _dot(sb_ref[...], wco_ref[...])
    m = gated_pool + gate_conv * bb
    h = x + _dot(m.astype(BF16), wo_ref[...])
    h_ref[...] = h.reshape(S, Lt, D)
    xn = h * lax.rsqrt(jnp.mean(h * h, axis=-1, keepdims=True) + NORM_EPS) * g2_ref[...]
    for s in range(SUBLANES):
        xn_ref[pl.ds(s, T, stride=SUBLANES), :] = xn[:, s * LANES:(s + 1) * LANES]

    logits = _dot(xn.astype(BF16), wr_ref[...]) + br_ref[...]
    lane_e = lax.broadcasted_iota(I32, (T, N_EXPERTS), 1).astype(F32)
    work = logits
    sels, tops, ids = [], [], []
    for _ in range(TOP_K):
        mx = jnp.max(work, axis=-1, keepdims=True)
        idx = jnp.min(jnp.where(work == mx, lane_e, float(N_EXPERTS)), axis=-1, keepdims=True)
        sel = lane_e == idx
        work = jnp.where(sel, -jnp.inf, work)
        sels.append(sel)
        tops.append(mx)
        ids.append(idx)
    exps = [jnp.exp(t - tops[0]) for t in tops]
    denom = exps[0] + exps[1] + exps[2] + exps[3]
    wts = [e / denom for e in exps]

    mask = jnp.zeros((T, N_EXPERTS), F32)
    for sel in sels:
        mask = mask + sel.astype(F32)
    tri = (lax.broadcasted_iota(I32, (T, T), 1) < lax.broadcasted_iota(I32, (T, T), 0)).astype(BF16)
    excl = _dot(tri, mask.astype(BF16)) + run_ref[...]
    run_new = run_ref[...] + jnp.sum(mask, axis=0, keepdims=True)
    run_ref[...] = run_new
    cnt_ref[...] = run_new

    lane_m = lax.broadcasted_iota(I32, (T, META_LANES), 1)
    meta = jnp.zeros((T, META_LANES), F32)
    for k in range(TOP_K):
        rank = jnp.sum(jnp.where(sels[k], excl, 0.0), axis=-1, keepdims=True)
        meta = jnp.where(lane_m == k, ids[k], meta)
        meta = jnp.where(lane_m == META_RANK + k, rank, meta)
        meta = jnp.where(lane_m == META_WEIGHT + k, wts[k], meta)
    meta_ref[...] = meta.reshape(S, Lt, META_LANES)


def _const_spec(shape):
    nd = len(shape)
    return pl.BlockSpec(shape, lambda b, l: (0,) * nd, pipeline_mode=pl.Buffered(1))


def _front(x, hist_pool, hist_conv, base_cnt, weights, S, Lt, pos0):
    B, L, D = x.shape
    P, C = POOL_WIDTH, CONV_DIM
    grid = (B // S, L // Lt)
    T = S * Lt
    tok = lambda b, l: (b, l, 0)
    per_b = lambda b, l: (b, 0, 0)
    in_specs = [
        pl.BlockSpec((S, Lt, D), tok),
        pl.BlockSpec((S, POOL_HIST, P), per_b),
        pl.BlockSpec((S, CONV_HIST, C), per_b),
        _const_spec((1, N_EXPERTS)),
    ] + [_const_spec(w.shape) for w in weights]
    out_shape = (
        jax.ShapeDtypeStruct((B, L, D), F32),
        jax.ShapeDtypeStruct((B * L * SUBLANES, LANES), F32),
        jax.ShapeDtypeStruct((B, L, META_LANES), F32),
        jax.ShapeDtypeStruct((1, B, POOL_HIST, P), F32),
        jax.ShapeDtypeStruct((1, B, CONV_HIST, C), F32),
        jax.ShapeDtypeStruct((1, N_EXPERTS), F32),
    )
    steps_l = L // Lt
    out_specs = (
        pl.BlockSpec((S, Lt, D), tok),
        pl.BlockSpec((T * SUBLANES, LANES), lambda b, l: (b * steps_l + l, 0)),
        pl.BlockSpec((S, Lt, META_LANES), tok),
        pl.BlockSpec((1, S, POOL_HIST, P), lambda b, l: (0, b, 0, 0)),
        pl.BlockSpec((1, S, CONV_HIST, C), lambda b, l: (0, b, 0, 0)),
        pl.BlockSpec((1, N_EXPERTS), lambda b, l: (0, 0)),
    )
    scratch = [
        pltpu.VMEM((S, POOL_BASE + Lt, P), F32),
        pltpu.VMEM((SUBLANES, S, CONV_BASE + Lt, C), F32),
        pltpu.VMEM((T, C), BF16),
        pltpu.VMEM((1, N_EXPERTS), F32),
        pltpu.VMEM((len(POOL_WINDOWS), Lt, POOL_BASE + Lt), BF16),
    ]
    return pl.pallas_call(
        functools.partial(_front_kernel, S, Lt, pos0),
        grid=grid, in_specs=in_specs, out_specs=out_specs, out_shape=out_shape,
        scratch_shapes=scratch,
        compiler_params=pltpu.CompilerParams(
            dimension_semantics=("arbitrary", "arbitrary"), vmem_limit_bytes=VMEM_LIMIT),
        name="front",
    )(x, hist_pool, hist_conv, base_cnt, *weights)


def _dispatch_kernel(T, steps_a, n_tiles, pos_ref, zrow_ref, used_ref, xa_ref, xb_ref, xs_ref,
                     sem, zbuf, zsem):
    i = pl.program_id(0)

    @pl.when(i == 0)
    def _():
        zbuf[...] = jnp.zeros_like(zbuf)

        def fill(row0):
            row0 = pl.multiple_of(row0, EXPERT_TILE)
            return pltpu.make_async_copy(zbuf, xs_ref.at[pl.ds(row0, EXPERT_TILE)], zsem)

        def tail_start(j, carry):
            fill(j * EXPERT_TILE).start()
            return carry

        def tail_wait(j, carry):
            fill(j * EXPERT_TILE).wait()
            return carry

        for e in range(N_EXPERTS):
            pl.when(zrow_ref[e] >= 0)(lambda e=e: fill(zrow_ref[e]).start())
        lax.fori_loop(used_ref[0], n_tiles, tail_start, 0)
        for e in range(N_EXPERTS):
            pl.when(zrow_ref[e] >= 0)(lambda e=e: fill(zrow_ref[e]).wait())
        lax.fori_loop(used_ref[0], n_tiles, tail_wait, 0)

    def scatter(src_ref):
        def start(t, carry):
            for k in range(TOP_K):
                pltpu.make_async_copy(
                    src_ref.at[t], xs_ref.at[pos_ref[TOP_K * t + k]], sem).start(priority=k % 2)
            return carry

        def wait(t, carry):
            for k in range(TOP_K):
                pltpu.make_async_copy(src_ref.at[0], xs_ref.at[0], sem).wait()
            return carry

        lax.fori_loop(0, T, start, 0, unroll=8)
        lax.fori_loop(0, T, wait, 0, unroll=8)

    pl.when(i < steps_a)(lambda: scatter(xa_ref))
    pl.when(i >= steps_a)(lambda: scatter(xb_ref))


def _dispatch(pos_flat, zero_rows, used_tiles, xn_a, xn_b, n_tiles):
    T = TOKEN_TILE
    steps_a, steps_b = xn_a.shape[0] // T, xn_b.shape[0] // T
    tile = (SUBLANES, LANES)
    return pl.pallas_call(
        functools.partial(_dispatch_kernel, T, steps_a, n_tiles),
        grid=(steps_a + steps_b,),
        in_specs=[
            pl.BlockSpec((T * TOP_K,), lambda i: (i,), memory_space=pltpu.SMEM),
            pl.BlockSpec(memory_space=pltpu.SMEM),
            pl.BlockSpec(memory_space=pltpu.SMEM),
            pl.BlockSpec((T,) + tile, lambda i: (jnp.minimum(i, steps_a - 1), 0, 0)),
            pl.BlockSpec((T,) + tile, lambda i: (jnp.maximum(i - steps_a, 0), 0, 0)),
        ],
        out_specs=pl.BlockSpec(memory_space=pl.ANY),
        out_shape=jax.ShapeDtypeStruct((n_tiles * EXPERT_TILE,) + tile, F32),
        scratch_shapes=[pltpu.SemaphoreType.DMA(()), pltpu.VMEM((EXPERT_TILE,) + tile, F32),
                        pltpu.SemaphoreType.DMA(())],
        compiler_params=pltpu.CompilerParams(dimension_semantics=("arbitrary",)),
        name="dispatch",
    )(pos_flat, zero_rows, used_tiles, xn_a, xn_b)


def _expert_kernel(te_ref, on_ref, last_ref, next_ref, slot_ref,
                   xs_ref, wgu_hbm, bgu_ref, wd_hbm, bd_ref, ys_ref,
                   wgu_f32, wd_f32, wgu_bf, wd_bf, sem_gu, sem_d):
    del last_ref
    i = pl.program_id(0)
    e = te_ref[i]
    slot = slot_ref[i]
    group_start = (i == 0) | (e != te_ref[jnp.maximum(i - 1, 0)])

    def weight_copies(expert, s):
        return (pltpu.make_async_copy(wgu_hbm.at[expert], wgu_f32.at[s], sem_gu.at[s]),
                pltpu.make_async_copy(wd_hbm.at[expert], wd_f32.at[s], sem_d.at[s]))

    @pl.when(i == 0)
    def _():
        for c in weight_copies(e, slot):
            c.start()

    @pl.when(group_start & (on_ref[i] == 1))
    def _():
        for c in weight_copies(e, slot):
            c.wait()

        @pl.when(next_ref[i] >= 0)
        def _():
            for c in weight_copies(next_ref[i], 1 - slot):
                c.start()

        wgu_bf[...] = wgu_f32[slot].astype(BF16)
        wd_bf[...] = wd_f32[slot].astype(BF16)

    @pl.when(on_ref[i] == 1)
    def _():
        xb = jnp.concatenate(
            [xs_ref[pl.ds(s, EXPERT_TILE, stride=SUBLANES), :] for s in range(SUBLANES)],
            axis=-1).astype(BF16)
        gu = _dot(xb, wgu_bf[...]) + bgu_ref[0]
        gate = jnp.minimum(gu[:, :D_FF], SWIGLU_LIMIT)
        up = jnp.clip(gu[:, D_FF:], -SWIGLU_LIMIT, SWIGLU_LIMIT)
        hm = (up + 1.0) * (gate * _sigmoid(SWIGLU_ALPHA * gate))
        y = _dot(hm.astype(BF16), wd_bf[...]) + bd_ref[0]
        for s in range(SUBLANES):
            ys_ref[pl.ds(s, EXPERT_TILE, stride=SUBLANES), :] = y[:, s * LANES:(s + 1) * LANES]

    @pl.when(on_ref[i] == 0)
    def _():
        ys_ref[...] = jnp.zeros_like(ys_ref)


def _experts(tile_expert, tile_on, last_tile, tile_next, tile_slot, xs, w_gate_up, b_gate_up,
             w_down, b_down):
    D = D_MODEL
    R = xs.shape[0] // SUBLANES
    TM = EXPERT_TILE
    row = lambda i, te, on, last, nxt, slot: (jnp.minimum(i, last[0]), 0)
    per_e = lambda i, te, on, last, nxt, slot: (te[i], 0, 0)
    grid_spec = pltpu.PrefetchScalarGridSpec(
        num_scalar_prefetch=5,
        grid=(R // TM,),
        in_specs=[
            pl.BlockSpec((TM * SUBLANES, LANES), row),
            pl.BlockSpec(memory_space=pl.ANY),
            pl.BlockSpec((1, 1, 2 * D_FF), per_e),
            pl.BlockSpec(memory_space=pl.ANY),
            pl.BlockSpec((1, 1, D), per_e),
        ],
        out_specs=pl.BlockSpec((TM * SUBLANES, LANES), lambda i, te, on, last, nxt, slot: (i, 0)),
        scratch_shapes=[
            pltpu.VMEM((2, D, 2 * D_FF), F32), pltpu.VMEM((2, D_FF, D), F32),
            pltpu.VMEM((D, 2 * D_FF), BF16), pltpu.VMEM((D_FF, D), BF16),
            pltpu.SemaphoreType.DMA((2,)), pltpu.SemaphoreType.DMA((2,)),
        ],
    )
    return pl.pallas_call(
        _expert_kernel,
        grid_spec=grid_spec,
        out_shape=jax.ShapeDtypeStruct((R * SUBLANES, LANES), F32),
        compiler_params=pltpu.CompilerParams(
            dimension_semantics=("arbitrary",), vmem_limit_bytes=VMEM_LIMIT),
        name="experts",
    )(tile_expert, tile_on, last_tile, tile_next, tile_slot, xs, w_gate_up,
      b_gate_up.reshape(N_EXPERTS, 1, 2 * D_FF), w_down, b_down.reshape(N_EXPERTS, 1, D))


def _combine_kernel(T, n_steps, pos_ref, pos_next_ref, meta_ref, h_ref, g_ref, ys_ref, out_ref,
                    gbuf, sem):
    i = pl.program_id(0)
    slot = i % 2

    def gather(rows_ref, s):
        def start(t, carry):
            row = pl.multiple_of(t * SUBLANES, SUBLANES)
            for k in range(TOP_K):
                pltpu.make_async_copy(
                    ys_ref.at[rows_ref[TOP_K * t + k]], gbuf.at[s, k, pl.ds(row, SUBLANES)],
                    sem.at[s]).start(priority=k % 2)
            return carry

        lax.fori_loop(0, T, start, 0, unroll=8)

    pl.when(i == 0)(lambda: gather(pos_ref, 0))
    pl.when(i + 1 < n_steps)(lambda: gather(pos_next_ref, 1 - slot))

    def wait(t, carry):
        for k in range(TOP_K):
            pltpu.make_async_copy(
                ys_ref.at[0], gbuf.at[slot, k, pl.ds(0, SUBLANES)], sem.at[slot]).wait()
        return carry

    lax.fori_loop(0, T, wait, 0, unroll=8)
    y = h_ref[...]
    for k in range(TOP_K):
        wk = meta_ref[:, META_WEIGHT + k:META_WEIGHT + k + 1]
        gk = jnp.concatenate(
            [gbuf[slot, k, pl.ds(s, T, stride=SUBLANES), :] for s in range(SUBLANES)], axis=-1)
        y = y + wk * gk
    out_ref[...] = y * lax.rsqrt(jnp.mean(y * y, axis=-1, keepdims=True) + NORM_EPS) * g_ref[...]


def _combine(pos_flat, meta, h, final_g, ys):
    N, D = h.shape
    T = TOKEN_TILE
    n_steps = N // T
    return pl.pallas_call(
        functools.partial(_combine_kernel, T, n_steps),
        grid=(n_steps,),
        in_specs=[
            pl.BlockSpec((T * TOP_K,), lambda i: (i,), memory_space=pltpu.SMEM),
            pl.BlockSpec((T * TOP_K,), lambda i: (jnp.minimum(i + 1, n_steps - 1),),
                         memory_space=pltpu.SMEM),
            pl.BlockSpec((T, META_LANES), lambda i: (i, 0)),
            pl.BlockSpec((T, D), lambda i: (i, 0)),
            pl.BlockSpec((1, D), lambda i: (0, 0)),
            pl.BlockSpec(memory_space=pl.ANY),
        ],
        out_specs=pl.BlockSpec((T, D), lambda i: (i, 0)),
        out_shape=jax.ShapeDtypeStruct((N, D), F32),
        scratch_shapes=[pltpu.VMEM((2, TOP_K, T * SUBLANES, LANES), F32),
                        pltpu.SemaphoreType.DMA((2,))],
        compiler_params=pltpu.CompilerParams(
            dimension_semantics=("arbitrary",), vmem_limit_bytes=VMEM_LIMIT),
        name="combine",
    )(pos_flat, pos_flat, meta, h, final_g.reshape(1, D), ys)


def _sorted_positions(meta, group_start):
    ids = meta[:, 0:TOP_K].astype(I32)
    rank = meta[:, META_RANK:META_RANK + TOP_K].astype(I32)
    experts = jnp.arange(N_EXPERTS, dtype=I32)
    start = jnp.sum(jnp.where(ids[:, :, None] == experts, group_start, 0), axis=-1)
    return (start + rank).reshape(-1)


def kernel(x_prompt, x_sample, state_pool, state_conv, norm1_g, w_in, w_pool_grp, pool_scale, w_pool_out, dw_w, dw_b, conv_ln_g, conv_ln_b, w_conv_out, w_out, norm2_g, w_router, b_router, w_gate_up, b_gate_up, w_down, b_down, final_g):
    depth = norm1_g.shape[0]
    assert depth == 1
    B, L, D = x_prompt.shape
    SB, SL, _ = x_sample.shape
    row = lambda v: v[0].reshape(1, -1)
    weights = (
        row(norm1_g), w_in[0].astype(BF16), w_pool_grp[0].astype(BF16), row(pool_scale),
        w_pool_out[0].astype(BF16),
        jnp.broadcast_to(dw_w[0][:, None, :], (CONV_WIDTH, SUBLANES, CONV_DIM)),
        row(dw_b), row(conv_ln_g), row(conv_ln_b),
        w_conv_out[0].astype(BF16), w_out[0].astype(BF16), row(norm2_g),
        w_router[0].astype(BF16), row(b_router),
    )
    zero_pool = jnp.zeros((B, POOL_HIST, POOL_WIDTH), F32)
    zero_conv = jnp.zeros((B, CONV_HIST, CONV_DIM), F32)
    zero_cnt = jnp.zeros((1, N_EXPERTS), F32)

    h_p, xn_p, meta_p, pool_p, conv_p, cnt_p = _front(
        x_prompt, zero_pool, zero_conv, zero_cnt, weights, 1, FRONT_TILE, 0)
    h_s, xn_s, meta_s, pool_s, conv_s, cnt_s = _front(
        x_sample, state_pool[0], state_conv[0], cnt_p, weights, SAMPLE_STREAMS, SL, L)

    n_tok = B * L + SB * SL
    n_tiles = (n_tok * TOP_K + N_EXPERTS * (EXPERT_TILE - 1)) // EXPERT_TILE
    counts = cnt_s[0].astype(I32)
    tiles_per_e = (counts + EXPERT_TILE - 1) // EXPERT_TILE
    tile_end = jnp.cumsum(tiles_per_e)
    group_start = (tile_end - tiles_per_e) * EXPERT_TILE
    tile_ids = jnp.arange(n_tiles, dtype=I32)
    tile_expert = jnp.minimum(
        jnp.sum((tile_ids[:, None] >= tile_end[None, :]).astype(I32), axis=1), N_EXPERTS - 1)
    tile_on = (tile_ids < tile_end[-1]).astype(I32)
    last_tile = (tile_end[-1:] - 1).astype(I32)
    expert_ids = jnp.arange(N_EXPERTS, dtype=I32)
    has_rows = tiles_per_e > 0
    later = (expert_ids[None, :] > expert_ids[:, None]) & has_rows[None, :]
    next_e = jnp.min(jnp.where(later, expert_ids[None, :], N_EXPERTS), axis=1)
    next_e = jnp.where(next_e < N_EXPERTS, next_e, -1).astype(I32)
    slot_e = ((jnp.cumsum(has_rows.astype(I32)) - 1) % 2).astype(I32)
    of_tile = tile_expert[:, None] == expert_ids[None, :]
    tile_next = jnp.sum(jnp.where(of_tile, next_e[None, :], 0), axis=1).astype(I32)
    tile_slot = jnp.sum(jnp.where(of_tile, slot_e[None, :], 0), axis=1).astype(I32)

    meta_p2 = meta_p.reshape(B * L, META_LANES)
    meta_s2 = meta_s.reshape(SB * SL, META_LANES)
    pos_p = _sorted_positions(meta_p2, group_start)
    pos_s = _sorted_positions(meta_s2, group_start)

    zero_rows = jnp.where(tiles_per_e > 0, (tile_end - 1) * EXPERT_TILE, -1).astype(I32)
    xs = _dispatch(jnp.concatenate([pos_p, pos_s]), zero_rows, tile_end[-1:].astype(I32),
                   xn_p.reshape(B * L, SUBLANES, LANES), xn_s.reshape(SB * SL, SUBLANES, LANES),
                   n_tiles)
    xs = xs.reshape(n_tiles * EXPERT_TILE * SUBLANES, LANES)
    ys = _experts(tile_expert, tile_on, last_tile, tile_next, tile_slot, xs,
                  w_gate_up[0], b_gate_up[0], w_down[0], b_down[0])
    ys = ys.reshape(n_tiles * EXPERT_TILE, SUBLANES, LANES)
    y_p = _combine(pos_p, meta_p2, h_p.reshape(B * L, D), final_g, ys)
    y_s = _combine(pos_s, meta_s2, h_s.reshape(SB * SL, D), final_g, ys)
    return (y_p.reshape(B, L, D), y_s.reshape(SB, SL, D), pool_p, conv_p, pool_s, conv_s)
```

```python
import functools

import jax
import jax.numpy as jnp
from jax import lax
from jax.experimental import pallas as pl
from jax.experimental.pallas import tpu as pltpu

F32 = jnp.float32
BF16 = jnp.bfloat16
I32 = jnp.int32

D_MODEL = 1024
POOL_WINDOWS = (2, 4, 8, 16)
POOL_WIDTH = 512
POOL_GROUP_DIM = 128
POOL_HIST = 15
CONV_DIM = 1024
CONV_WIDTH = 31
CONV_HIST = 30
N_EXPERTS = 32
TOP_K = 4
D_FF = 1024
SWIGLU_LIMIT = 7.0
SWIGLU_ALPHA = 1.702
NORM_EPS = 1e-6

SUBLANES = 8
LANES = 128
POOL_BASE = 16
CONV_BASE = 32
META_LANES = 128
META_RANK = 4
META_WEIGHT = 8

FRONT_TILE = 256
SAMPLE_STREAMS = 8
TOKEN_TILE = 256
EXPERT_TILE = 384
CONV_CHUNK_ROWS = 16
VMEM_LIMIT = 56 * 1024 * 1024


def _sigmoid(v):
    return 1.0 / (1.0 + jnp.exp(-v))


def _dot(a, b):
    return jnp.dot(a, b, preferred_element_type=F32)


def _front_kernel(S, Lt, pos0,
                  x_ref, hp_ref, hc_ref, base_ref, g1_ref, win_ref, wg_ref, ps_ref, wpo_ref,
                  dww_ref, dwb_ref, lng_ref, lnb_ref, wco_ref, wo_ref, g2_ref, wr_ref, br_ref,
                  h_ref, xn_ref, meta_ref, np_ref, nc_ref, cnt_ref,
                  ep_ref, ec_ref, sb_ref, run_ref, band_ref):
    b = pl.program_id(0)
    l = pl.program_id(1)
    T = S * Lt
    P, C, D = POOL_WIDTH, CONV_DIM, D_MODEL
    tap0 = CONV_BASE - CONV_HIST
    first = l == 0

    @pl.when((b == 0) & first)
    def _():
        run_ref[...] = base_ref[...]
        col = lax.broadcasted_iota(I32, (Lt, POOL_BASE + Lt), 1)
        newest = lax.broadcasted_iota(I32, (Lt, POOL_BASE + Lt), 0) + POOL_BASE
        for g, w in enumerate(POOL_WINDOWS):
            band_ref[g] = ((col <= newest) & (col > newest - w)).astype(BF16)
        ep_ref[...] = jnp.zeros_like(ep_ref)
        ec_ref[...] = jnp.zeros_like(ec_ref)

    def conv_half():
        rc = min(CONV_CHUNK_ROWS, Lt)
        sc = CONV_CHUNK_ROWS // rc
        for s0 in range(0, S, sc):
            for r0 in range(0, Lt, rc):
                acc = None
                for j in range(CONV_WIDTH):
                    r = (tap0 + j) % SUBLANES
                    base = tap0 + j - r + r0
                    rows = ec_ref[r, s0:s0 + sc, base:base + rc, :]
                    term = rows.reshape(sc * rc // SUBLANES, SUBLANES, C) * dww_ref[j]
                    acc = term if acc is None else acc + term
                v = acc + dwb_ref[...]
                mu = jnp.mean(v, axis=-1, keepdims=True)
                vc = v - mu
                var = jnp.mean(vc * vc, axis=-1, keepdims=True)
                vn = vc * lax.rsqrt(var + NORM_EPS) * lng_ref[...] + lnb_ref[...]
                act = vn * _sigmoid(vn)
                row = s0 * Lt + r0
                sb_ref[row:row + sc * rc, :] = act.reshape(sc * rc, C).astype(BF16)

    def projection_half():
        ep_ref[:, POOL_BASE - POOL_HIST:POOL_BASE, :] = jnp.where(
            first, hp_ref[...], ep_ref[:, Lt + POOL_BASE - POOL_HIST:Lt + POOL_BASE, :])
        for r in range(SUBLANES):
            lo = max(0, tap0 - r)
            ec_ref[r, :, lo:CONV_BASE - r, :] = jnp.where(
                first, hc_ref[:, lo + r - tap0:CONV_HIST, :],
                ec_ref[r, :, Lt + lo:Lt + CONV_BASE - r, :])

        x = x_ref[...].reshape(T, D)
        n1 = x * lax.rsqrt(jnp.mean(x * x, axis=-1, keepdims=True) + NORM_EPS) * g1_ref[...]
        n1b = n1.astype(BF16)

        zp = _dot(n1b, win_ref[:, 0:P])
        ep_ref[:, POOL_BASE:POOL_BASE + Lt, :] = zp.reshape(S, Lt, P)
        np_ref[0] = ep_ref[:, Lt + POOL_BASE - POOL_HIST:Lt + POOL_BASE, :]
        pos = (pos0 + l * Lt + lax.broadcasted_iota(I32, (Lt, 1), 0)).astype(F32)
        ds = [[] for _ in POOL_WINDOWS]
        for s in range(S):
            ext = ep_ref[s]
            hi = ext.astype(BF16)
            lo_part = (ext - hi.astype(F32)).astype(BF16)
            for g, w in enumerate(POOL_WINDOWS):
                sl = slice(g * POOL_GROUP_DIM, (g + 1) * POOL_GROUP_DIM)
                parts = _dot(band_ref[g], jnp.concatenate([hi[:, sl], lo_part[:, sl]], axis=-1))
                total = parts[:, :POOL_GROUP_DIM] + parts[:, POOL_GROUP_DIM:]
                cnt = jnp.minimum(pos + 1.0, float(w))
                ds[g].append(total / cnt - ext[POOL_BASE:, sl])
        ys = []
        for g in range(len(POOL_WINDOWS)):
            d = ds[g][0] if S == 1 else jnp.concatenate(ds[g], axis=0)
            ys.append(_dot(d.astype(BF16), wg_ref[g]))
        yp = jnp.concatenate(ys, axis=-1) * ps_ref[...]
        a = _dot(yp.astype(BF16), wpo_ref[...])

        za = _dot(n1b, win_ref[:, P:P + C])
        zg = _dot(n1b, win_ref[:, P + C:P + 2 * C])
        u3 = (za * _sigmoid(zg)).reshape(S, Lt, C)
        for r in range(SUBLANES):
            ec_ref[r, :, CONV_BASE - r:CONV_BASE - r + Lt, :] = u3
        nc_ref[0] = ec_ref[tap0, :, Lt:Lt + CONV_HIST, :]

        ga = _sigmoid(_dot(n1b, win_ref[:, P + 2 * C:P + 2 * C + D]))
        gb = _sigmoid(_dot(n1b, win_ref[:, P + 2 * C + D:P + 2 * C + 2 * D]))
        return x, ga * a, gb

    x, gated_pool, gate_conv = projection_half()
    conv_half()
    _front_tail(S, Lt, x, gated_pool, gate_conv, sb_ref, wco_ref, wo_ref, g2_ref, wr_ref, br_ref,
                run_ref, h_ref, xn_ref, meta_ref, cnt_ref)


def _front_tail(S, Lt, x, gated_pool, gate_conv, sb_ref, wco_ref, wo_ref, g2_ref, wr_ref, br_ref,
                run_ref, h_ref, xn_ref, meta_ref, cnt_ref):
    T = S * Lt
    D = D_MODEL
    bb = _dot(sb_ref[...], wco_ref[...])
    m = gated_pool + gate_conv * bb
    h = x + _dot(m.astype(BF16), wo_ref[...])
    h_ref[...] = h.reshape(S, Lt, D)
    xn = h * lax.rsqrt(jnp.mean(h * h, axis=-1, keepdims=True) + NORM_EPS) * g2_ref[...]
    for s in range(SUBLANES):
        xn_ref[pl.ds(s, T, stride=SUBLANES), :] = xn[:, s * LANES:(s + 1) * LANES]

    logits = _dot(xn.astype(BF16), wr_ref[...]) + br_ref[...]
    lane_e = lax.broadcasted_iota(I32, (T, N_EXPERTS), 1).astype(F32)
    work = logits
    sels, tops, ids = [], [], []
    for _ in range(TOP_K):
        mx = jnp.max(work, axis=-1, keepdims=True)
        idx = jnp.min(jnp.where(work == mx, lane_e, float(N_EXPERTS)), axis=-1, keepdims=True)
        sel = lane_e == idx
        work = jnp.where(sel, -jnp.inf, work)
        sels.append(sel)
        tops.append(mx)
        ids.append(idx)
    exps = [jnp.exp(t - tops[0]) for t in tops]
    denom = exps[0] + exps[1] + exps[2] + exps[3]
    wts = [e / denom for e in exps]

    mask = jnp.zeros((T, N_EXPERTS), F32)
    for sel in sels:
        mask = mask + sel.astype(F32)
    tri = (lax.broadcasted_iota(I32, (T, T), 1) < lax.broadcasted_iota(I32, (T, T), 0)).astype(BF16)
    excl = _dot(tri, mask.astype(BF16)) + run_ref[...]
    run_new = run_ref[...] + jnp.sum(mask, axis=0, keepdims=True)
    run_ref[...] = run_new
    cnt_ref[...] = run_new

    lane_m = lax.broadcasted_iota(I32, (T, META_LANES), 1)
    meta = jnp.zeros((T, META_LANES), F32)
    for k in range(TOP_K):
        rank = jnp.sum(jnp.where(sels[k], excl, 0.0), axis=-1, keepdims=True)
        meta = jnp.where(lane_m == k, ids[k], meta)
        meta = jnp.where(lane_m == META_RANK + k, rank, meta)
        meta = jnp.where(lane_m == META_WEIGHT + k, wts[k], meta)
    meta_ref[...] = meta.reshape(S, Lt, META_LANES)


def _const_spec(shape):
    nd = len(shape)
    return pl.BlockSpec(shape, lambda b, l: (0,) * nd, pipeline_mode=pl.Buffered(1))


def _front(x, hist_pool, hist_conv, base_cnt, weights, S, Lt, pos0):
    B, L, D = x.shape
    P, C = POOL_WIDTH, CONV_DIM
    grid = (B // S, L // Lt)
    T = S * Lt
    tok = lambda b, l: (b, l, 0)
    per_b = lambda b, l: (b, 0, 0)
    in_specs = [
        pl.BlockSpec((S, Lt, D), tok),
        pl.BlockSpec((S, POOL_HIST, P), per_b),
        pl.BlockSpec((S, CONV_HIST, C), per_b),
        _const_spec((1, N_EXPERTS)),
    ] + [_const_spec(w.shape) for w in weights]
    out_shape = (
        jax.ShapeDtypeStruct((B, L, D), F32),
        jax.ShapeDtypeStruct((B * L * SUBLANES, LANES), F32),
        jax.ShapeDtypeStruct((B, L, META_LANES), F32),
        jax.ShapeDtypeStruct((1, B, POOL_HIST, P), F32),
        jax.ShapeDtypeStruct((1, B, CONV_HIST, C), F32),
        jax.ShapeDtypeStruct((1, N_EXPERTS), F32),
    )
    steps_l = L // Lt
    out_specs = (
        pl.BlockSpec((S, Lt, D), tok),
        pl.BlockSpec((T * SUBLANES, LANES), lambda b, l: (b * steps_l + l, 0)),
        pl.BlockSpec((S, Lt, META_LANES), tok),
        pl.BlockSpec((1, S, POOL_HIST, P), lambda b, l: (0, b, 0, 0)),
        pl.BlockSpec((1, S, CONV_HIST, C), lambda b, l: (0, b, 0, 0)),
        pl.BlockSpec((1, N_EXPERTS), lambda b, l: (0, 0)),
    )
    scratch = [
        pltpu.VMEM((S, POOL_BASE + Lt, P), F32),
        pltpu.VMEM((SUBLANES, S, CONV_BASE + Lt, C), F32),
        pltpu.VMEM((T, C), BF16),
        pltpu.VMEM((1, N_EXPERTS), F32),
        pltpu.VMEM((len(POOL_WINDOWS), Lt, POOL_BASE + Lt), BF16),
    ]
    return pl.pallas_call(
        functools.partial(_front_kernel, S, Lt, pos0),
        grid=grid, in_specs=in_specs, out_specs=out_specs, out_shape=out_shape,
        scratch_shapes=scratch,
        compiler_params=pltpu.CompilerParams(
            dimension_semantics=("arbitrary", "arbitrary"), vmem_limit_bytes=VMEM_LIMIT),
        name="front",
    )(x, hist_pool, hist_conv, base_cnt, *weights)


def _dispatch_kernel(T, steps_a, n_tiles, pos_ref, zrow_ref, used_ref, xa_ref, xb_ref, xs_ref,
                     sem, zbuf, zsem):
    i = pl.program_id(0)

    @pl.when(i == 0)
    def _():
        zbuf[...] = jnp.zeros_like(zbuf)

        def fill(row0):
            row0 = pl.multiple_of(row0, EXPERT_TILE)
            return pltpu.make_async_copy(zbuf, xs_ref.at[pl.ds(row0, EXPERT_TILE)], zsem)

        def tail_start(j, carry):
            fill(j * EXPERT_TILE).start()
            return carry

        def tail_wait(j, carry):
            fill(j * EXPERT_TILE).wait()
            return carry

        for e in range(N_EXPERTS):
            pl.when(zrow_ref[e] >= 0)(lambda e=e: fill(zrow_ref[e]).start())
        lax.fori_loop(used_ref[0], n_tiles, tail_start, 0)
        for e in range(N_EXPERTS):
            pl.when(zrow_ref[e] >= 0)(lambda e=e: fill(zrow_ref[e]).wait())
        lax.fori_loop(used_ref[0], n_tiles, tail_wait, 0)

    def scatter(src_ref):
        def start(t, carry):
            for k in range(TOP_K):
                pltpu.make_async_copy(
                    src_ref.at[t], xs_ref.at[pos_ref[TOP_K * t + k]], sem).start(priority=k % 2)
            return carry

        def wait(t, carry):
            for k in range(TOP_K):
                pltpu.make_async_copy(src_ref.at[0], xs_ref.at[0], sem).wait()
            return carry

        lax.fori_loop(0, T, start, 0, unroll=8)
        lax.fori_loop(0, T, wait, 0, unroll=8)

    pl.when(i < steps_a)(lambda: scatter(xa_ref))
    pl.when(i >= steps_a)(lambda: scatter(xb_ref))


def _dispatch(pos_flat, zero_rows, used_tiles, xn_a, xn_b, n_tiles):
    T = TOKEN_TILE
    steps_a, steps_b = xn_a.shape[0] // T, xn_b.shape[0] // T
    tile = (SUBLANES, LANES)
    return pl.pallas_call(
        functools.partial(_dispatch_kernel, T, steps_a, n_tiles),
        grid=(steps_a + steps_b,),
        in_specs=[
            pl.BlockSpec((T * TOP_K,), lambda i: (i,), memory_space=pltpu.SMEM),
            pl.BlockSpec(memory_space=pltpu.SMEM),
            pl.BlockSpec(memory_space=pltpu.SMEM),
            pl.BlockSpec((T,) + tile, lambda i: (jnp.minimum(i, steps_a - 1), 0, 0)),
            pl.BlockSpec((T,) + tile, lambda i: (jnp.maximum(i - steps_a, 0), 0, 0)),
        ],
        out_specs=pl.BlockSpec(memory_space=pl.ANY),
        out_shape=jax.ShapeDtypeStruct((n_tiles * EXPERT_TILE,) + tile, F32),
        scratch_shapes=[pltpu.SemaphoreType.DMA(()), pltpu.VMEM((EXPERT_TILE,) + tile, F32),
                        pltpu.SemaphoreType.DMA(())],
        compiler_params=pltpu.CompilerParams(dimension_semantics=("arbitrary",)),
        name="dispatch",
    )(pos_flat, zero_rows, used_tiles, xn_a, xn_b)


def _expert_kernel(te_ref, on_ref, last_ref, next_ref, slot_ref,
                   xs_ref, wgu_hbm, bgu_ref, wd_hbm, bd_ref, ys_ref,
                   wgu_f32, wd_f32, wgu_bf, wd_bf, sem_gu, sem_d):
    del last_ref
    i = pl.program_id(0)
    e = te_ref[i]
    slot = slot_ref[i]
    group_start = (i == 0) | (e != te_ref[jnp.maximum(i - 1, 0)])

    def weight_copies(expert, s):
        return (pltpu.make_async_copy(wgu_hbm.at[expert], wgu_f32.at[s], sem_gu.at[s]),
                pltpu.make_async_copy(wd_hbm.at[expert], wd_f32.at[s], sem_d.at[s]))

    @pl.when(i == 0)
    def _():
        for c in weight_copies(e, slot):
            c.start()

    @pl.when(group_start & (on_ref[i] == 1))
    def _():
        for c in weight_copies(e, slot):
            c.wait()

        @pl.when(next_ref[i] >= 0)
        def _():
            for c in weight_copies(next_ref[i], 1 - slot):
                c.start()

        wgu_bf[...] = wgu_f32[slot].astype(BF16)
        wd_bf[...] = wd_f32[slot].astype(BF16)

    @pl.when(on_ref[i] == 1)
    def _():
        xb = jnp.concatenate(
            [xs_ref[pl.ds(s, EXPERT_TILE, stride=SUBLANES), :] for s in range(SUBLANES)],
            axis=-1).astype(BF16)
        gu = _dot(xb, wgu_bf[...]) + bgu_ref[0]
        gate = jnp.minimum(gu[:, :D_FF], SWIGLU_LIMIT)
        up = jnp.clip(gu[:, D_FF:], -SWIGLU_LIMIT, SWIGLU_LIMIT)
        hm = (up + 1.0) * (gate * _sigmoid(SWIGLU_ALPHA * gate))
        y = _dot(hm.astype(BF16), wd_bf[...]) + bd_ref[0]
        for s in range(SUBLANES):
            ys_ref[pl.ds(s, EXPERT_TILE, stride=SUBLANES), :] = y[:, s * LANES:(s + 1) * LANES]

    @pl.when(on_ref[i] == 0)
    def _():
        ys_ref[...] = jnp.zeros_like(ys_ref)


def _experts(tile_expert, tile_on, last_tile, tile_next, tile_slot, xs, w_gate_up, b_gate_up,
             w_down, b_down):
    D = D_MODEL
    R = xs.shape[0] // SUBLANES
    TM = EXPERT_TILE
    row = lambda i, te, on, last, nxt, slot: (jnp.minimum(i, last[0]), 0)
    per_e = lambda i, te, on, last, nxt, slot: (te[i], 0, 0)
    grid_spec = pltpu.PrefetchScalarGridSpec(
        num_scalar_prefetch=5,
        grid=(R // TM,),
        in_specs=[
            pl.BlockSpec((TM * SUBLANES, LANES), row),
            pl.BlockSpec(memory_space=pl.ANY),
            pl.BlockSpec((1, 1, 2 * D_FF), per_e),
            pl.BlockSpec(memory_space=pl.ANY),
            pl.BlockSpec((1, 1, D), per_e),
        ],
        out_specs=pl.BlockSpec((TM * SUBLANES, LANES), lambda i, te, on, last, nxt, slot: (i, 0)),
        scratch_shapes=[
            pltpu.VMEM((2, D, 2 * D_FF), F32), pltpu.VMEM((2, D_FF, D), F32),
            pltpu.VMEM((D, 2 * D_FF), BF16), pltpu.VMEM((D_FF, D), BF16),
            pltpu.SemaphoreType.DMA((2,)), pltpu.SemaphoreType.DMA((2,)),
        ],
    )
    return pl.pallas_call(
        _expert_kernel,
        grid_spec=grid_spec,
        out_shape=jax.ShapeDtypeStruct((R * SUBLANES, LANES), F32),
        compiler_params=pltpu.CompilerParams(
            dimension_semantics=("arbitrary",), vmem_limit_bytes=VMEM_LIMIT),
        name="experts",
    )(tile_expert, tile_on, last_tile, tile_next, tile_slot, xs, w_gate_up,
      b_gate_up.reshape(N_EXPERTS, 1, 2 * D_FF), w_down, b_down.reshape(N_EXPERTS, 1, D))


def _combine_kernel(T, n_steps, pos_ref, pos_next_ref, meta_ref, h_ref, g_ref, ys_ref, out_ref,
                    gbuf, sem):
    i = pl.program_id(0)
    slot = i % 2

    def gather(rows_ref, s, unroll):
        def start(t, carry):
            row = pl.multiple_of(t * SUBLANES, SUBLANES)
            for k in range(TOP_K):
                pltpu.make_async_copy(
                    ys_ref.at[rows_ref[TOP_K * t + k]], gbuf.at[s, k, pl.ds(row, SUBLANES)],
                    sem.at[s]).start(priority=k % 2)
            return carry

        lax.fori_loop(0, T, start, 0, unroll=unroll)

    def drain(s):
        def wait(t, carry):
            for k in range(TOP_K):
                pltpu.make_async_copy(
                    ys_ref.at[0], gbuf.at[s, k, pl.ds(0, SUBLANES)], sem.at[s]).wait()
            return carry

        lax.fori_loop(0, T, wait, 0, unroll=8)

    pl.when(i == 0)(lambda: gather(pos_ref, 0, 8))
    drain(slot)
    gather(pos_next_ref, 1 - slot, True)
    y = h_ref[...]
    for k in range(TOP_K):
        wk = meta_ref[:, META_WEIGHT + k:META_WEIGHT + k + 1]
        gk = jnp.concatenate(
            [gbuf[slot, k, pl.ds(s, T, stride=SUBLANES), :] for s in range(SUBLANES)], axis=-1)
        y = y + wk * gk
    out_ref[...] = y * lax.rsqrt(jnp.mean(y * y, axis=-1, keepdims=True) + NORM_EPS) * g_ref[...]
    pl.when(i + 1 == n_steps)(lambda: drain(1 - slot))


def _combine(pos_flat, meta, h, final_g, ys):
    N, D = h.shape
    T = TOKEN_TILE
    n_steps = N // T
    return pl.pallas_call(
        functools.partial(_combine_kernel, T, n_steps),
        grid=(n_steps,),
        in_specs=[
            pl.BlockSpec((T * TOP_K,), lambda i: (i,), memory_space=pltpu.SMEM),
            pl.BlockSpec((T * TOP_K,), lambda i: (jnp.minimum(i + 1, n_steps - 1),),
                         memory_space=pltpu.SMEM),
            pl.BlockSpec((T, META_LANES), lambda i: (i, 0)),
            pl.BlockSpec((T, D), lambda i: (i, 0)),
            pl.BlockSpec((1, D), lambda i: (0, 0)),
            pl.BlockSpec(memory_space=pl.ANY),
        ],
        out_specs=pl.BlockSpec((T, D), lambda i: (i, 0)),
        out_shape=jax.ShapeDtypeStruct((N, D), F32),
        scratch_shapes=[pltpu.VMEM((2, TOP_K, T * SUBLANES, LANES), F32),
                        pltpu.SemaphoreType.DMA((2,))],
        compiler_params=pltpu.CompilerParams(
            dimension_semantics=("arbitrary",), vmem_limit_bytes=VMEM_LIMIT),
        name="combine",
    )(pos_flat, pos_flat, meta, h, final_g.reshape(1, D), ys)


def _sorted_positions(meta, group_start):
    ids = meta[:, 0:TOP_K].astype(I32)
    rank = meta[:, META_RANK:META_RANK + TOP_K].astype(I32)
    experts = jnp.arange(N_EXPERTS, dtype=I32)
    start = jnp.sum(jnp.where(ids[:, :, None] == experts, group_start, 0), axis=-1)
    return (start + rank).reshape(-1)


def kernel(x_prompt, x_sample, state_pool, state_conv, norm1_g, w_in, w_pool_grp, pool_scale, w_pool_out, dw_w, dw_b, conv_ln_g, conv_ln_b, w_conv_out, w_out, norm2_g, w_router, b_router, w_gate_up, b_gate_up, w_down, b_down, final_g):
    depth = norm1_g.shape[0]
    assert depth == 1
    B, L, D = x_prompt.shape
    SB, SL, _ = x_sample.shape
    row = lambda v: v[0].reshape(1, -1)
    weights = (
        row(norm1_g), w_in[0].astype(BF16), w_pool_grp[0].astype(BF16), row(pool_scale),
        w_pool_out[0].astype(BF16),
        jnp.broadcast_to(dw_w[0][:, None, :], (CONV_WIDTH, SUBLANES, CONV_DIM)),
        row(dw_b), row(conv_ln_g), row(conv_ln_b),
        w_conv_out[0].astype(BF16), w_out[0].astype(BF16), row(norm2_g),
        w_router[0].astype(BF16), row(b_router),
    )
    zero_pool = jnp.zeros((B, POOL_HIST, POOL_WIDTH), F32)
    zero_conv = jnp.zeros((B, CONV_HIST, CONV_DIM), F32)
    zero_cnt = jnp.zeros((1, N_EXPERTS), F32)

    h_p, xn_p, meta_p, pool_p, conv_p, cnt_p = _front(
        x_prompt, zero_pool, zero_conv, zero_cnt, weights, 1, FRONT_TILE, 0)
    h_s, xn_s, meta_s, pool_s, conv_s, cnt_s = _front(
        x_sample, state_pool[0], state_conv[0], cnt_p, weights, SAMPLE_STREAMS, SL, L)

    n_tok = B * L + SB * SL
    n_tiles = (n_tok * TOP_K + N_EXPERTS * (EXPERT_TILE - 1)) // EXPERT_TILE
    counts = cnt_s[0].astype(I32)
    tiles_per_e = (counts + EXPERT_TILE - 1) // EXPERT_TILE
    tile_end = jnp.cumsum(tiles_per_e)
    group_start = (tile_end - tiles_per_e) * EXPERT_TILE
    tile_ids = jnp.arange(n_tiles, dtype=I32)
    tile_expert = jnp.minimum(
        jnp.sum((tile_ids[:, None] >= tile_end[None, :]).astype(I32), axis=1), N_EXPERTS - 1)
    tile_on = (tile_ids < tile_end[-1]).astype(I32)
    last_tile = (tile_end[-1:] - 1).astype(I32)
    expert_ids = jnp.arange(N_EXPERTS, dtype=I32)
    has_rows = tiles_per_e > 0
    later = (expert_ids[None, :] > expert_ids[:, None]) & has_rows[None, :]
    next_e = jnp.min(jnp.where(later, expert_ids[None, :], N_EXPERTS), axis=1)
    next_e = jnp.where(next_e < N_EXPERTS, next_e, -1).astype(I32)
    slot_e = ((jnp.cumsum(has_rows.astype(I32)) - 1) % 2).astype(I32)
    of_tile = tile_expert[:, None] == expert_ids[None, :]
    tile_next = jnp.sum(jnp.where(of_tile, next_e[None, :], 0), axis=1).astype(I32)
    tile_slot = jnp.sum(jnp.where(of_tile, slot_e[None, :], 0), axis=1).astype(I32)

    meta_p2 = meta_p.reshape(B * L, META_LANES)
    meta_s2 = meta_s.reshape(SB * SL, META_LANES)
    pos_p = _sorted_positions(meta_p2, group_start)
    pos_s = _sorted_positions(meta_s2, group_start)

    zero_rows = jnp.where(tiles_per_e > 0, (tile_end - 1) * EXPERT_TILE, -1).astype(I32)
    xs = _dispatch(jnp.concatenate([pos_p, pos_s]), zero_rows, tile_end[-1:].astype(I32),
                   xn_p.reshape(B * L, SUBLANES, LANES), xn_s.reshape(SB * SL, SUBLANES, LANES),
                   n_tiles)
    xs = xs.reshape(n_tiles * EXPERT_TILE * SUBLANES, LANES)
    ys = _experts(tile_expert, tile_on, last_tile, tile_next, tile_slot, xs,
                  w_gate_up[0], b_gate_up[0], w_down[0], b_down[0])
    ys = ys.reshape(n_tiles * EXPERT_TILE, SUBLANES, LANES)
    y_p = _combine(pos_p, meta_p2, h_p.reshape(B * L, D), final_g, ys)
    y_s = _combine(pos_s, meta_s2, h_s.reshape(SB * SL, D), final_g, ys)
    return (y_p.reshape(B, L, D), y_s.reshape(SB, SL, D), pool_p, conv_p, pool_s, conv_s)
```
